```python
import math
import jax, jax.numpy as jnp
from jax import lax
import numpy as np

D_MODEL = 2048
BATCH = 2
SEQ = 8192
DEPTH = 4

CHUNK = 64
Q_BLOCK = 128
EPS = 1e-6

DA_HEADS = 6
DA_QK = 64
DA_V = 128
MLA_HEADS = 6
MLA_Q_RANK = 512
MLA_KV_RANK = 256
MLA_NOPE = 128
MLA_ROPE = 64
MLA_V = 128
ROPE_THETA = 10000.0
SG_GROUPS = 4
SG_CH = 128
SG_LEN = 128
D_FF = 5632
CONV_W = 3

DA_WIDTH = DA_HEADS * DA_V
MLA_WIDTH = MLA_HEADS * MLA_V
SG_WIDTH = SG_GROUPS * SG_CH
D_MIX = DA_WIDTH + MLA_WIDTH + SG_WIDTH

SPLITS = (DA_HEADS * 2 * DA_QK, DA_HEADS * 2 * DA_QK, DA_WIDTH,
          MLA_Q_RANK, MLA_KV_RANK, MLA_ROPE, SG_WIDTH, SG_WIDTH)
IN_COLS = sum(SPLITS)
SPLIT_IDX = tuple(int(i) for i in np.cumsum(SPLITS)[:-1])
ALIBI_SLOPES = tuple(2.0 ** (-8.0 * (h + 1) / DA_HEADS) for h in range(DA_HEADS))

kernel_name = 'hymba_style_diffattn_mla_sgu_convffn_adaln'


def rms_norm(x, gain=None):
    xf = x.astype(jnp.float32)
    y = xf * lax.rsqrt(jnp.mean(xf * xf, axis=-1, keepdims=True) + EPS)
    if gain is not None:
        y = y * gain.astype(jnp.float32)
    return y.astype(x.dtype)


def rope_cos_sin(pos):
    half = MLA_ROPE // 2
    inv = ROPE_THETA ** (-jnp.arange(half, dtype=jnp.float32) / half)
    ang = pos.astype(jnp.float32)[..., None] * inv
    return jnp.cos(ang)[:, :, None, :], jnp.sin(ang)[:, :, None, :]


def apply_rope(x, cos, sin):
    xf = x.astype(jnp.float32)
    x1, x2 = jnp.split(xf, 2, axis=-1)
    return jnp.concatenate([x1 * cos - x2 * sin, x2 * cos + x1 * sin], axis=-1).astype(x.dtype)


def to_heads(t):
    return t.transpose(0, 2, 1, 3)


def chunk_causal_attention(qs, ks, v, coefs, alibi=None):
    B, H, S, _ = v.shape
    nb = S // Q_BLOCK
    scale = qs[0].shape[-1] ** -0.5
    key_chunk = jnp.arange(S) // CHUNK
    blocks = lambda t: t.reshape(B, H, nb, Q_BLOCK, t.shape[-1]).transpose(2, 0, 1, 3, 4)
    q_blocks = tuple(blocks(q) for q in qs)
    if alibi is not None:
        slopes, pos = alibi
        posf = pos.astype(jnp.float32)
        pos_blocks = posf.reshape(B, nb, Q_BLOCK).transpose(1, 0, 2)
    else:
        pos_blocks = jnp.zeros((nb, B, Q_BLOCK), jnp.float32)

    def step(args):
        qb, pb, ib = args
        q_chunk = (ib * Q_BLOCK + jnp.arange(Q_BLOCK)) // CHUNK
        mask = key_chunk[None, :] <= q_chunk[:, None]
        bias = 0.0
        if alibi is not None:
            dist = jnp.abs(pb[:, :, None] - posf[:, None, :])
            bias = -slopes[None, :, None, None] * dist[:, None]
        probs = 0.0
        for q_i, k_i, c_i in zip(qb, ks, coefs):
            s = jnp.einsum('bhqd,bhkd->bhqk', q_i, k_i).astype(jnp.float32) * scale + bias
            s = jnp.where(mask, s, -jnp.inf)
            probs = probs + c_i * jax.nn.softmax(s, axis=-1)
        return jnp.einsum('bhqk,bhkd->bhqd', probs.astype(v.dtype), v)

    out = lax.map(step, (q_blocks, pos_blocks, jnp.arange(nb)))
    return out.transpose(1, 2, 0, 3, 4).reshape(B, H, S, v.shape[-1])


def diff_attention(q, k, v, pos, slopes, q_gain, k_gain, lq1, lk1, lq2, lk2, head_gain, lam_init):
    B, S, _ = q.shape
    q = rms_norm(q.reshape(B, S, DA_HEADS, 2, DA_QK), q_gain)
    k = rms_norm(k.reshape(B, S, DA_HEADS, 2, DA_QK), k_gain)
    v = v.reshape(B, S, DA_HEADS, DA_V)
    lam = (jnp.exp(jnp.sum(lq1 * lk1)) - jnp.exp(jnp.sum(lq2 * lk2)) + lam_init).astype(jnp.float32)
    o = chunk_causal_attention(
        (to_heads(q[..., 0, :]), to_heads(q[..., 1, :])),
        (to_heads(k[..., 0, :]), to_heads(k[..., 1, :])),
        to_heads(v), (1.0, -lam), alibi=(slopes, pos))
    o = rms_norm(o, head_gain[:, None, :]) * (1.0 - lam_init)
    return o.transpose(0, 2, 1, 3).reshape(B, S, DA_WIDTH)


def latent_attention(q_a, kv_a, k_rope, cos, sin, q_a_gain, w_uq, kv_a_gain, w_ukv, q_gain, k_gain):
    B, S, _ = q_a.shape
    q = (rms_norm(q_a, q_a_gain) @ w_uq).reshape(B, S, MLA_HEADS, MLA_NOPE + MLA_ROPE)
    kv = (rms_norm(kv_a, kv_a_gain) @ w_ukv).reshape(B, S, MLA_HEADS, MLA_NOPE + MLA_V)
    k_nope, v = kv[..., :MLA_NOPE], kv[..., MLA_NOPE:]
    k_r = jnp.broadcast_to(k_rope[:, :, None, :], (B, S, MLA_HEADS, MLA_ROPE))
    k = jnp.concatenate([k_nope, k_r], axis=-1)
    q = rms_norm(q, q_gain)
    k = rms_norm(k, k_gain)
    q = jnp.concatenate([q[..., :MLA_NOPE], apply_rope(q[..., MLA_NOPE:], cos, sin)], axis=-1)
    k = jnp.concatenate([k[..., :MLA_NOPE], apply_rope(k[..., MLA_NOPE:], cos, sin)], axis=-1)
    o = chunk_causal_attention((to_heads(q),), (to_heads(k),), to_heads(v), (1.0,))
    return o.transpose(0, 2, 1, 3).reshape(B, S, MLA_WIDTH)


def spatial_gating(u, v, v_gain, w_s, b_s):
    B, S, _ = u.shape
    u = jax.nn.gelu(u).reshape(B, S, SG_GROUPS, SG_CH)
    v = rms_norm(jax.nn.gelu(v).reshape(B, S, SG_GROUPS, SG_CH), v_gain)
    v = v.reshape(B, S // SG_LEN, SG_LEN, SG_GROUPS, SG_CH)
    w = w_s * jnp.tril(jnp.ones((SG_LEN, SG_LEN), w_s.dtype))
    s = jnp.einsum('gts,bnsgc->bntgc', w, v) + b_s.T[None, None, :, :, None]
    return (u * s.reshape(B, S, SG_GROUPS, SG_CH)).reshape(B, S, SG_WIDTH)


def conv_ffn(h, w_up, conv_w, conv_b, w_down):
    a = h @ w_up
    S = a.shape[1]
    ap = jnp.pad(a, ((0, 0), (CONV_W - 1, 0), (0, 0)))
    y = conv_b + a * conv_w[CONV_W - 1]
    for j in range(CONV_W - 1):
        y = y + ap[:, j:j + S] * conv_w[j]
    g, val = jnp.split(y, 2, axis=-1)
    return (jax.nn.silu(g) * val) @ w_down


def setup_inputs(seed: int = 0) -> dict:
    key = jax.random.key(seed)
    keys = jax.random.split(key, 32)
    counter = [0]

    def nxt():
        k = keys[counter[0]]
        counter[0] += 1
        return k

    def nrm(shape, scale):
        return jax.random.normal(nxt(), shape, jnp.float32) * scale

    def gain(shape):
        return 1.0 + nrm(shape, 0.02)

    L, D = DEPTH, D_MODEL
    x = nrm((BATCH, SEQ, D), 1.0)
    c = nrm((BATCH, D), 1.0)
    offset = jax.random.randint(nxt(), (BATCH, 1), 0, 1024, jnp.int32)
    positions = offset + jnp.arange(SEQ, dtype=jnp.int32)[None, :]
    return {
        'x': x,
        'c': c,
        'positions': positions,
        'w_ada': nrm((L, D, 6 * D), 0.01),
        'b_ada': nrm((L, 6 * D), 0.01),
        'w_in': nrm((L, D, IN_COLS), D ** -0.5),
        'da_q_gain': gain((L, DA_QK)),
        'da_k_gain': gain((L, DA_QK)),
        'da_lq1': nrm((L, DA_QK), 0.1),
        'da_lk1': nrm((L, DA_QK), 0.1),
        'da_lq2': nrm((L, DA_QK), 0.1),
        'da_lk2': nrm((L, DA_QK), 0.1),
        'da_head_gain': gain((L, DA_HEADS, DA_V)),
        'mla_q_a_gain': gain((L, MLA_Q_RANK)),
        'mla_w_uq': nrm((L, MLA_Q_RANK, MLA_HEADS * (MLA_NOPE + MLA_ROPE)), MLA_Q_RANK ** -0.5),
        'mla_kv_a_gain': gain((L, MLA_KV_RANK)),
        'mla_w_ukv': nrm((L, MLA_KV_RANK, MLA_HEADS * (MLA_NOPE + MLA_V)), MLA_KV_RANK ** -0.5),
        'mla_q_gain': gain((L, MLA_NOPE + MLA_ROPE)),
        'mla_k_gain': gain((L, MLA_NOPE + MLA_ROPE)),
        'sg_v_gain': gain((L, SG_GROUPS, SG_CH)),
        'sg_w': nrm((L, SG_GROUPS, SG_LEN, SG_LEN), SG_LEN ** -0.5),
        'sg_b': gain((L, SG_GROUPS, SG_LEN)),
        'w_out': nrm((L, D_MIX, D), D_MIX ** -0.5),
        'ffn_w_up': nrm((L, D, 2 * D_FF), D ** -0.5),
        'ffn_conv_w': nrm((L, CONV_W, 2 * D_FF), CONV_W ** -0.5),
        'ffn_conv_b': nrm((L, 2 * D_FF), 0.01),
        'ffn_w_down': nrm((L, D_FF, D), D_FF ** -0.5),
    }


def reference(x, c, positions, w_ada, b_ada, w_in, da_q_gain, da_k_gain, da_lq1, da_lk1,
              da_lq2, da_lk2, da_head_gain, mla_q_a_gain, mla_w_uq, mla_kv_a_gain, mla_w_ukv,
              mla_q_gain, mla_k_gain, sg_v_gain, sg_w, sg_b, w_out, ffn_w_up, ffn_conv_w,
              ffn_conv_b, ffn_w_down):
    cond = jax.nn.silu(c)
    slopes = jnp.asarray(ALIBI_SLOPES, jnp.float32)
    cos, sin = rope_cos_sin(positions)
    for l in range(DEPTH):
        mod = (cond @ w_ada[l] + b_ada[l])[:, None, :]
        sh1, sc1, g1, sh2, sc2, g2 = jnp.split(mod, 6, axis=-1)
        h = rms_norm(x) * (1.0 + sc1) + sh1
        z = h @ w_in[l]
        da_q, da_k, da_v, q_a, kv_a, k_rope, sg_u, sg_v = jnp.split(z, SPLIT_IDX, axis=-1)
        lam_init = 0.8 - 0.6 * math.exp(-0.3 * l)
        out_a = diff_attention(da_q, da_k, da_v, positions, slopes, da_q_gain[l], da_k_gain[l],
                               da_lq1[l], da_lk1[l], da_lq2[l], da_lk2[l], da_head_gain[l], lam_init)
        out_b = latent_attention(q_a, kv_a, k_rope, cos, sin, mla_q_a_gain[l], mla_w_uq[l],
                                 mla_kv_a_gain[l], mla_w_ukv[l], mla_q_gain[l], mla_k_gain[l])
        out_c = spatial_gating(sg_u, sg_v, sg_v_gain[l], sg_w[l], sg_b[l])
        mix = jnp.concatenate([out_a, out_b, out_c], axis=-1) @ w_out[l]
        x = x + g1 * mix
        h = rms_norm(x) * (1.0 + sc2) + sh2
        x = x + g2 * conv_ffn(h, ffn_w_up[l], ffn_conv_w[l], ffn_conv_b[l], ffn_w_down[l])
    return x
```

```python
import functools
import math

import jax
import jax.numpy as jnp
from jax import lax
from jax.experimental import pallas as pl
from jax.experimental.pallas import tpu as pltpu

CHUNK = 64
EPS = 1e-6
DA_HEADS = 6
DA_QK = 64
DA_V = 128
MLA_HEADS = 6
MLA_Q_RANK = 512
MLA_KV_RANK = 256
MLA_NOPE = 128
MLA_ROPE = 64
MLA_V = 128
ROPE_THETA = 10000.0
SG_GROUPS = 4
SG_CH = 128
SG_LEN = 128
CONV_W = 3
DA_WIDTH = DA_HEADS * DA_V
MLA_WIDTH = MLA_HEADS * MLA_V
SG_WIDTH = SG_GROUPS * SG_CH
ALIBI_SLOPES = tuple(2.0 ** (-8.0 * (h + 1) / DA_HEADS) for h in range(DA_HEADS))

LOG2E = 1.4426950408889634
LANES = 128
QK_PAD = 256
NEG_BIG = -1e30
HALO = 16
NORM_ROWS = 256
VMEM_LIMIT = 56 * 1024 * 1024

C_QA, C_SGU, C_SGV, C_DAQ, C_DAK, C_DAV, C_KVA, C_KR, IN_PAD = (
    0, 512, 1024, 1536, 2304, 3072, 3840, 4096, 4608)

BF16 = jnp.bfloat16
F32 = jnp.float32


def _cp(*sem):
    return pltpu.CompilerParams(dimension_semantics=sem, vmem_limit_bytes=VMEM_LIMIT)


def _dot(a, b):
    return jnp.dot(a, b, preferred_element_type=F32)


def _dot_nt(a, b):
    return lax.dot_general(a, b, (((1,), (1,)), ((), ())), preferred_element_type=F32)


def _rms_scale(x, n):
    return lax.rsqrt(jnp.sum(x * x, axis=-1, keepdims=True) * (1.0 / n) + EPS)


def _ada_kernel(c_ref, w_ref, b_ref, o_ref):
    c = c_ref[...]
    cond = c / (1.0 + jnp.exp(-c))
    o_ref[0] = jnp.dot(cond, w_ref[0], preferred_element_type=F32,
                       precision=lax.Precision.HIGHEST) + b_ref[0]


def _ada_mod(c, w_ada, b_ada):
    L, D, N = w_ada.shape
    B = c.shape[0]
    tn = 1024
    return pl.pallas_call(
        _ada_kernel,
        grid=(L, N // tn),
        in_specs=[pl.BlockSpec((B, D), lambda l, j: (0, 0)),
                  pl.BlockSpec((1, D, tn), lambda l, j: (l, 0, j)),
                  pl.BlockSpec((1, 1, tn), lambda l, j: (l, 0, j))],
        out_specs=pl.BlockSpec((1, B, tn), lambda l, j: (l, 0, j)),
        out_shape=jax.ShapeDtypeStruct((L, B, N), F32),
        compiler_params=_cp("parallel", "parallel"),
        name="ada_mod",
    )(c, w_ada, b_ada.reshape(L, 1, N))


def _rope_table_kernel(pos_ref, inv_ref, c_ref, s1_ref, s2_ref):
    ang = pos_ref[...] * inv_ref[...]
    lane = lax.broadcasted_iota(jnp.int32, ang.shape, 1)
    half = MLA_ROPE // 2
    cos = jnp.cos(ang)
    sin = jnp.sin(ang)
    c_ref[...] = jnp.where(lane < MLA_ROPE, cos, 0.0)
    s1_ref[...] = jnp.where(lane < half, -sin, 0.0)
    s2_ref[...] = jnp.where((lane >= half) & (lane < MLA_ROPE), sin, 0.0)


def _rope_tables(pos_col):
    T = pos_col.shape[0]
    half = MLA_ROPE // 2
    inv = ROPE_THETA ** (-jnp.arange(half, dtype=F32) / half)
    inv_row = jnp.concatenate([inv, inv, jnp.zeros((LANES - MLA_ROPE,), F32)]).reshape(1, LANES)
    tm = min(T, 2048)
    sds = jax.ShapeDtypeStruct((T, LANES), F32)
    spec = pl.BlockSpec((tm, LANES), lambda i: (i, 0))
    return pl.pallas_call(
        _rope_table_kernel,
        grid=(T // tm,),
        in_specs=[pl.BlockSpec((tm, 1), lambda i: (i, 0)),
                  pl.BlockSpec((1, LANES), lambda i: (0, 0))],
        out_specs=[spec, spec, spec],
        out_shape=[sds, sds, sds],
        compiler_params=_cp("parallel"),
        name="rope_tables",
    )(pos_col, inv_row)


def _norm_matmul_kernel(x_ref, sh_ref, sc_ref, w_ref, o_ref, h_scr):
    @pl.when(pl.program_id(1) == 0)
    def _():
        for r in range(0, x_ref.shape[0], NORM_ROWS):
            x = x_ref[r:r + NORM_ROWS, :]
            h = x * _rms_scale(x, x.shape[-1]) * (1.0 + sc_ref[0]) + sh_ref[0]
            h_scr[r:r + NORM_ROWS, :] = h.astype(BF16)

    o_ref[...] = _dot(h_scr[...], w_ref[...]).astype(o_ref.dtype)


def _norm_matmul(x2, mod_l, which, w, S, tm, tn):
    T, D = x2.shape
    N = w.shape[1]
    tpb = S // tm
    return pl.pallas_call(
        _norm_matmul_kernel,
        grid=(T // tm, N // tn),
        in_specs=[pl.BlockSpec((tm, D), lambda i, j: (i, 0)),
                  pl.BlockSpec((1, 1, D), lambda i, j: (i // tpb, 0, which)),
                  pl.BlockSpec((1, 1, D), lambda i, j: (i // tpb, 0, which + 1)),
                  pl.BlockSpec((D, tn), lambda i, j: (0, j))],
        out_specs=pl.BlockSpec((tm, tn), lambda i, j: (i, j)),
        out_shape=jax.ShapeDtypeStruct((T, N), BF16),
        scratch_shapes=[pltpu.VMEM((tm, D), BF16)],
        compiler_params=_cp("parallel", "arbitrary"),
        name="norm_matmul",
    )(x2, mod_l, mod_l, w)


def _da_prep_kernel(zq_ref, zk_ref, d_ref, gq_ref, gk_ref, qaug_ref, q_ref, k_ref):
    tm = zq_ref.shape[0]
    lane = lax.broadcasted_iota(jnp.int32, (1, LANES), 1)
    lo = lane < DA_QK

    def qk_norm(z, g):
        sq = z * z
        ss_all = jnp.sum(sq, axis=-1, keepdims=True)
        ss_lo = jnp.sum(jnp.where(lo, sq, 0.0), axis=-1, keepdims=True)
        inv_lo = lax.rsqrt(ss_lo * (1.0 / DA_QK) + EPS)
        inv_hi = lax.rsqrt((ss_all - ss_lo) * (1.0 / DA_QK) + EPS)
        return z * jnp.where(lo, inv_lo, inv_hi) * g

    d = d_ref[...]
    d_hi = jnp.floor(d * (1.0 / LANES)) * LANES
    d_lo = d - d_hi
    kaug = jnp.where(lane < 3, d_hi, jnp.where(lane < 6, d_lo, 0.0)).astype(BF16)
    for h in range(DA_HEADS):
        sl = slice(h * LANES, (h + 1) * LANES)
        qn = qk_norm(zq_ref[:, sl].astype(F32), gq_ref[...])
        kn = qk_norm(zk_ref[:, sl].astype(F32), gk_ref[...])
        qa = jnp.broadcast_to(qaug_ref[h], (tm, LANES)).astype(BF16)
        q_ref[0, h, 0, :, :LANES] = jnp.where(lo, qn, 0.0).astype(BF16)
        q_ref[0, h, 0, :, LANES:] = qa
        q_ref[0, h, 1, :, :LANES] = jnp.where(lo, 0.0, qn).astype(BF16)
        q_ref[0, h, 1, :, LANES:] = qa
        k_ref[0, h, :, :LANES] = kn.astype(BF16)
        k_ref[0, h, :, LANES:] = kaug


def _da_prep(z, d_col, gq, gk, qaug, B, S, tm):
    T = z.shape[0]
    tpb = S // tm
    H = DA_HEADS
    return pl.pallas_call(
        _da_prep_kernel,
        grid=(T // tm,),
        in_specs=[pl.BlockSpec((tm, DA_WIDTH), lambda i: (i, C_DAQ // DA_WIDTH)),
                  pl.BlockSpec((tm, DA_WIDTH), lambda i: (i, C_DAK // DA_WIDTH)),
                  pl.BlockSpec((tm, 1), lambda i: (i, 0)),
                  pl.BlockSpec((1, LANES), lambda i: (0, 0)),
                  pl.BlockSpec((1, LANES), lambda i: (0, 0)),
                  pl.BlockSpec((H, 1, LANES), lambda i: (0, 0, 0))],
        out_specs=[pl.BlockSpec((1, H, 2, tm, QK_PAD), lambda i: (i // tpb, 0, 0, i % tpb, 0)),
                   pl.BlockSpec((1, H, tm, QK_PAD), lambda i: (i // tpb, 0, i % tpb, 0))],
        out_shape=[jax.ShapeDtypeStruct((B, H, 2, S, QK_PAD), BF16),
                   jax.ShapeDtypeStruct((B, H, S, QK_PAD), BF16)],
        compiler_params=_cp("parallel"),
        name="da_prep",
    )(z, z, d_col, gq, gk, qaug)


def _mla_prep_kernel(zqa_ref, zkva_ref, zkr_ref, gqa_ref, gkva_ref, wqn_ref, wqr_ref, wkn_ref, wv_ref,
                     gqn_ref, gqr_ref, gkn_ref, gkr_ref, c_ref, s1_ref, s2_ref, q_ref, k_ref, v_ref):
    half = MLA_ROPE // 2
    width = MLA_NOPE + MLA_ROPE
    cos, sin1, sin2 = c_ref[...], s1_ref[...], s2_ref[...]

    def rope(t):
        return (t * cos + pltpu.roll(t, LANES - half, 1) * sin1 + pltpu.roll(t, half, 1) * sin2)

    zqa = zqa_ref[...].astype(F32)
    qa = (zqa * _rms_scale(zqa, MLA_Q_RANK) * gqa_ref[...]).astype(BF16)
    zkva = zkva_ref[...].astype(F32)
    kva = (zkva * _rms_scale(zkva, MLA_KV_RANK) * gkva_ref[...]).astype(BF16)
    qn_all = _dot(qa, wqn_ref[...])
    qr_all = _dot(qa, wqr_ref[...])
    kn_all = _dot(kva, wkn_ref[...])
    v_ref[...] = _dot(kva, wv_ref[...]).astype(BF16)
    kr = zkr_ref[...].astype(F32)
    kr_ss = jnp.sum(kr * kr, axis=-1, keepdims=True)
    kr_rot = rope(kr * gkr_ref[...])
    for h in range(MLA_HEADS):
        sl = slice(h * LANES, (h + 1) * LANES)
        qn, qr, kn = qn_all[:, sl], qr_all[:, sl], kn_all[:, sl]
        q_inv = lax.rsqrt((jnp.sum(qn * qn, axis=-1, keepdims=True)
                           + jnp.sum(qr * qr, axis=-1, keepdims=True)) * (1.0 / width) + EPS)
        k_inv = lax.rsqrt((jnp.sum(kn * kn, axis=-1, keepdims=True) + kr_ss) * (1.0 / width) + EPS)
        q_ref[0, h, :, :LANES] = (qn * q_inv * gqn_ref[...]).astype(BF16)
        q_ref[0, h, :, LANES:] = rope(qr * q_inv * gqr_ref[...]).astype(BF16)
        k_ref[0, h, :, :LANES] = (kn * k_inv * gkn_ref[...]).astype(BF16)
        k_ref[0, h, :, LANES:] = (kr_rot * k_inv).astype(BF16)


def _mla_prep(z, gqa, gkva, wqn, wqr, wkn, wv, gqn, gqr, gkn, gkr, tabs, B, S, tm):
    T = z.shape[0]
    tpb = S // tm
    H = MLA_HEADS
    full = lambda a: pl.BlockSpec(a.shape, lambda i: (0,) * a.ndim)
    row = pl.BlockSpec((tm, LANES), lambda i: (i, 0))
    return pl.pallas_call(
        _mla_prep_kernel,
        grid=(T // tm,),
        in_specs=[pl.BlockSpec((tm, MLA_Q_RANK), lambda i: (i, C_QA // MLA_Q_RANK)),
                  pl.BlockSpec((tm, MLA_KV_RANK), lambda i: (i, C_KVA // MLA_KV_RANK)),
                  pl.BlockSpec((tm, LANES), lambda i: (i, C_KR // LANES)),
                  full(gqa), full(gkva), full(wqn), full(wqr), full(wkn), full(wv),
                  full(gqn), full(gqr), full(gkn), full(gkr), row, row, row],
        out_specs=[pl.BlockSpec((1, H, tm, QK_PAD), lambda i: (i // tpb, 0, i % tpb, 0)),
                   pl.BlockSpec((1, H, tm, QK_PAD), lambda i: (i // tpb, 0, i % tpb, 0)),
                   pl.BlockSpec((tm, MLA_WIDTH), lambda i: (i, 0))],
        out_shape=[jax.ShapeDtypeStruct((B, H, S, QK_PAD), BF16),
                   jax.ShapeDtypeStruct((B, H, S, QK_PAD), BF16),
                   jax.ShapeDtypeStruct((T, MLA_WIDTH), BF16)],
        compiler_params=_cp("parallel"),
        name="mla_prep",
    )(z, z, z, gqa, gkva, wqn, wqr, wkn, wv, gqn, gqr, gkn, gkr, *tabs)


def _gelu_tanh(x):
    return 0.5 * x * (1.0 + jnp.tanh(0.7978845608028654 * (x + 0.044715 * (x * x * x))))


def _sg_kernel(zu_ref, zv_ref, g_ref, w_ref, b_ref, o_ref):
    tm = zu_ref.shape[0]
    r = lax.broadcasted_iota(jnp.int32, (SG_LEN, SG_LEN), 0)
    c = lax.broadcasted_iota(jnp.int32, (SG_LEN, SG_LEN), 1)
    causal = c <= r
    for g in range(SG_GROUPS):
        sl = slice(g * SG_CH, (g + 1) * SG_CH)
        wt = jnp.where(causal, w_ref[g], 0.0).astype(BF16)
        v = _gelu_tanh(zv_ref[:, sl].astype(F32))
        v = (v * _rms_scale(v, SG_CH) * g_ref[:, sl]).astype(BF16)
        u = _gelu_tanh(zu_ref[:, sl].astype(F32))
        for n in range(tm // SG_LEN):
            rows = slice(n * SG_LEN, (n + 1) * SG_LEN)
            s = _dot(wt, v[rows, :]) + b_ref[:, sl]
            o_ref[rows, sl] = (u[rows, :] * s).astype(BF16)


def _sg(z, gain, w, bias, tm):
    T = z.shape[0]
    return pl.pallas_call(
        _sg_kernel,
        grid=(T // tm,),
        in_specs=[pl.BlockSpec((tm, SG_WIDTH), lambda i: (i, C_SGU // SG_WIDTH)),
                  pl.BlockSpec((tm, SG_WIDTH), lambda i: (i, C_SGV // SG_WIDTH)),
                  pl.BlockSpec((1, SG_WIDTH), lambda i: (0, 0)),
                  pl.BlockSpec((SG_GROUPS, SG_LEN, SG_LEN), lambda i: (0, 0, 0)),
                  pl.BlockSpec((SG_LEN, SG_WIDTH), lambda i: (0, 0))],
        out_specs=pl.BlockSpec((tm, SG_WIDTH), lambda i: (i, 0)),
        out_shape=jax.ShapeDtypeStruct((T, SG_WIDTH), BF16),
        compiler_params=_cp("parallel"),
        name="spatial_gating",
    )(z, z, gain, w, bias)


def _flash(q, k_at, v_at, n_off, diag_tiles, tk):
    rows = q.shape[0]

    def step(start, carry, fix=None):
        m, l, acc = carry
        s = _dot_nt(q, k_at(start))
        if fix is not None:
            s = fix(s)
        m_new = jnp.maximum(m, jnp.max(s, axis=-1, keepdims=True))
        alpha = jnp.exp2(m - m_new)
        p = jnp.exp2(s - m_new)
        l = alpha * l + jnp.sum(p, axis=-1, keepdims=True)
        acc = alpha * acc + _dot(p.astype(BF16), v_at(start))
        return m_new, l, acc

    carry = (jnp.full((rows, 1), NEG_BIG, F32), jnp.zeros((rows, 1), F32),
             jnp.zeros((rows, DA_V), F32))
    carry = lax.fori_loop(0, n_off, lambda j, c: step(pl.multiple_of(j * tk, tk), c), carry)
    for start, fix in diag_tiles:
        carry = step(start, carry, fix)
    _, l, acc = carry
    return acc / l


def _chunk_mask(rows, tk, tq, col_offset):
    r = lax.broadcasted_iota(jnp.int32, (rows, tk), 0)
    c = lax.broadcasted_iota(jnp.int32, (rows, tk), 1)
    r = jnp.where(r >= tq, r - tq, r)
    return lax.shift_right_logical(c + col_offset, 6) <= lax.shift_right_logical(r, 6)


def _da_attn_kernel(q_ref, k_ref, v_ref, dq_ref, dk_ref, sl2_ref, lam_ref, hg_ref, o_ref, *, tq, lam_init):
    qi = pl.program_id(2)
    q = q_ref[0, 0].reshape(2 * tq, QK_PAD)
    base = pl.multiple_of(qi * tq, tq)
    dq = dq_ref[...]
    dq2 = jnp.concatenate([dq, dq], axis=0)
    corr = sl2_ref[0, :, :1] * jnp.maximum(dk_ref[0] - dq2, 0.0)
    visible = _chunk_mask(2 * tq, tq, tq, 0)

    o = _flash(q,
               lambda s: k_ref[0, 0, pl.ds(s, tq), :],
               lambda s: v_ref[pl.ds(s, tq), :],
               qi, [(base, lambda s: jnp.where(visible, s - corr, NEG_BIG))], tq)
    lv = lam_ref[...]
    lam = (jnp.exp(jnp.sum(lv[0:1] * lv[1:2], axis=-1, keepdims=True))
           - jnp.exp(jnp.sum(lv[2:3] * lv[3:4], axis=-1, keepdims=True)) + lam_init)
    od = o[:tq] - lam * o[tq:]
    od = od * _rms_scale(od, DA_V) * hg_ref[0] * (1.0 - lam_init)
    o_ref[...] = od.astype(o_ref.dtype)


def _da_attention(q, k, z, d_col, d_row, sl2, lam_vecs, hg, B, S, tq, lam_init):
    H = DA_HEADS
    nq = S // tq
    return pl.pallas_call(
        functools.partial(_da_attn_kernel, tq=tq, lam_init=lam_init),
        grid=(B, H, nq),
        in_specs=[pl.BlockSpec((1, 1, 2, tq, QK_PAD), lambda b, h, i: (b, h, 0, i, 0)),
                  pl.BlockSpec((1, 1, S, QK_PAD), lambda b, h, i: (b, h, 0, 0)),
                  pl.BlockSpec((S, DA_V), lambda b, h, i: (b, C_DAV // DA_V + h)),
                  pl.BlockSpec((tq, 1), lambda b, h, i: (b * nq + i, 0)),
                  pl.BlockSpec((1, 1, tq), lambda b, h, i: (b, 0, i)),
                  pl.BlockSpec((1, 1, LANES), lambda b, h, i: (h, 0, 0)),
                  pl.BlockSpec((4, DA_QK), lambda b, h, i: (0, 0)),
                  pl.BlockSpec((1, 1, DA_V), lambda b, h, i: (h, 0, 0))],
        out_specs=pl.BlockSpec((tq, DA_V), lambda b, h, i: (b * nq + i, h)),
        out_shape=jax.ShapeDtypeStruct((B * S, DA_WIDTH), BF16),
        compiler_params=_cp("parallel", "parallel", "parallel"),
        name="da_attention",
    )(q, k, z, d_col, d_row, sl2, lam_vecs, hg)


def _mla_attn_kernel(q_ref, k_ref, v_ref, o_ref, *, tq, tk):
    qi = pl.program_id(2)
    base = pl.multiple_of(qi * tq, tq)
    diag = []
    for t in range(tq // tk):
        visible = _chunk_mask(tq, tk, tq, t * tk)
        diag.append((base + t * tk, lambda s, vis=visible: jnp.where(vis, s, NEG_BIG)))
    o = _flash(q_ref[0, 0],
               lambda s: k_ref[0, 0, pl.ds(s, tk), :],
               lambda s: v_ref[pl.ds(s, tk), :],
               qi * (tq // tk), diag, tk)
    o_ref[...] = o.astype(o_ref.dtype)


def _mla_attention(q, k, v, B, S, tq, tk):
    H = MLA_HEADS
    nq = S // tq
    return pl.pallas_call(
        functools.partial(_mla_attn_kernel, tq=tq, tk=tk),
        grid=(B, H, nq),
        in_specs=[pl.BlockSpec((1, 1, tq, QK_PAD), lambda b, h, i: (b, h, i, 0)),
                  pl.BlockSpec((1, 1, S, QK_PAD), lambda b, h, i: (b, h, 0, 0)),
                  pl.BlockSpec((S, MLA_V), lambda b, h, i: (b, h))],
        out_specs=pl.BlockSpec((tq, MLA_V), lambda b, h, i: (b * nq + i, h)),
        out_shape=jax.ShapeDtypeStruct((B * S, MLA_WIDTH), BF16),
        compiler_params=_cp("parallel", "parallel", "parallel"),
        name="mla_attention",
    )(q, k, v)


def _out_proj_kernel(a_ref, b_ref, c_ref, wa_ref, wb_ref, wc_ref, x_ref, g_ref, o_ref):
    mix = _dot(a_ref[...], wa_ref[...]) + _dot(b_ref[...], wb_ref[...]) + _dot(c_ref[...], wc_ref[...])
    o_ref[...] = x_ref[...] + g_ref[0] * mix


def _out_proj(a, b, c, w, x2, mod_l, S, tm, tn):
    T, D = x2.shape
    tpb = S // tm
    nb = D // tn
    return pl.pallas_call(
        _out_proj_kernel,
        grid=(T // tm, nb),
        in_specs=[pl.BlockSpec((tm, DA_WIDTH), lambda i, j: (i, 0)),
                  pl.BlockSpec((tm, MLA_WIDTH), lambda i, j: (i, 0)),
                  pl.BlockSpec((tm, SG_WIDTH), lambda i, j: (i, 0)),
                  pl.BlockSpec((DA_WIDTH, tn), lambda i, j: (0, j)),
                  pl.BlockSpec((MLA_WIDTH, tn), lambda i, j: (1, j)),
                  pl.BlockSpec((SG_WIDTH, tn), lambda i, j: ((DA_WIDTH + MLA_WIDTH) // SG_WIDTH, j)),
                  pl.BlockSpec((tm, tn), lambda i, j: (i, j)),
                  pl.BlockSpec((1, 1, tn), lambda i, j: (i // tpb, 0, 2 * nb + j))],
        out_specs=pl.BlockSpec((tm, tn), lambda i, j: (i, j)),
        out_shape=jax.ShapeDtypeStruct((T, D), F32),
        compiler_params=_cp("parallel", "parallel"),
        name="out_proj",
    )(a, b, c, w, w, w, x2, mod_l)


def _ffn_kernel(x_ref, xh_ref, sh_ref, sc_ref, g_ref, wg_ref, wv_ref, cwg_ref, cwv_ref, cbg_ref, cbv_ref,
                wd_ref, o_ref, h_scr, acc_scr, *, tpb):
    i = pl.program_id(0)
    j = pl.program_id(1)

    @pl.when(j == 0)
    def _():
        def modnorm(x):
            return x * _rms_scale(x, x.shape[-1]) * (1.0 + sc_ref[0]) + sh_ref[0]

        for r in range(0, x_ref.shape[0], NORM_ROWS):
            h_scr[HALO + r:HALO + r + NORM_ROWS, :] = modnorm(x_ref[r:r + NORM_ROWS, :]).astype(BF16)
        keep = jnp.where(i % tpb == 0, 0.0, 1.0)
        h_scr[:HALO, :] = (modnorm(xh_ref[...]) * keep).astype(BF16)
        acc_scr[...] = jnp.zeros_like(acc_scr)

    h = h_scr[...]

    def conv(a, cw_ref, cb_ref):
        y = (cb_ref[...] + a * cw_ref[2:3, :] + pltpu.roll(a, 1, 0) * cw_ref[1:2, :]
             + pltpu.roll(a, 2, 0) * cw_ref[0:1, :])
        return y[HALO:, :]

    yg = conv(_dot(h, wg_ref[...]), cwg_ref, cbg_ref)
    yv = conv(_dot(h, wv_ref[...]), cwv_ref, cbv_ref)
    act = (yg / (1.0 + jnp.exp(-yg)) * yv).astype(BF16)
    acc_scr[...] += _dot(act, wd_ref[...])

    @pl.when(j == pl.num_programs(1) - 1)
    def _():
        o_ref[...] = x_ref[...] + g_ref[0] * acc_scr[...]


def _ffn(x2, mod_l, w_up, conv_w, conv_b, w_down, S, tm, tf):
    T, D = x2.shape
    F = w_down.shape[0]
    nf = F // tf
    tpb = S // tm
    hb = tm // HALO
    return pl.pallas_call(
        functools.partial(_ffn_kernel, tpb=tpb),
        grid=(T // tm, nf),
        in_specs=[pl.BlockSpec((tm, D), lambda i, j: (i, 0)),
                  pl.BlockSpec((HALO, D), lambda i, j: (jnp.maximum(i * hb - 1, 0), 0)),
                  pl.BlockSpec((1, 1, D), lambda i, j: (i // tpb, 0, 3)),
                  pl.BlockSpec((1, 1, D), lambda i, j: (i // tpb, 0, 4)),
                  pl.BlockSpec((1, 1, D), lambda i, j: (i // tpb, 0, 5)),
                  pl.BlockSpec((D, tf), lambda i, j: (0, j)),
                  pl.BlockSpec((D, tf), lambda i, j: (0, nf + j)),
                  pl.BlockSpec((CONV_W, tf), lambda i, j: (0, j)),
                  pl.BlockSpec((CONV_W, tf), lambda i, j: (0, nf + j)),
                  pl.BlockSpec((1, tf), lambda i, j: (0, j)),
                  pl.BlockSpec((1, tf), lambda i, j: (0, nf + j)),
                  pl.BlockSpec((tf, D), lambda i, j: (j, 0))],
        out_specs=pl.BlockSpec((tm, D), lambda i, j: (i, 0)),
        out_shape=jax.ShapeDtypeStruct((T, D), F32),
        scratch_shapes=[pltpu.VMEM((HALO + tm, D), BF16), pltpu.VMEM((tm, D), F32)],
        compiler_params=_cp("parallel", "arbitrary"),
        name="conv_ffn",
    )(x2, x2, mod_l, mod_l, mod_l, w_up, w_up, conv_w, conv_w, conv_b, conv_b, w_down)


def _split3(v):
    a = v.astype(BF16).astype(F32)
    b = (v - a).astype(BF16).astype(F32)
    c = (v - a - b).astype(BF16).astype(F32)
    return a, b, c


def kernel(x, c, positions, w_ada, b_ada, w_in, da_q_gain, da_k_gain, da_lq1, da_lk1, da_lq2, da_lk2, da_head_gain, mla_q_a_gain, mla_w_uq, mla_kv_a_gain, mla_w_ukv, mla_q_gain, mla_k_gain, sg_v_gain, sg_w, sg_b, w_out, ffn_w_up, ffn_conv_w, ffn_conv_b, ffn_w_down):
    B, S, D = x.shape
    L = w_ada.shape[0]
    T = B * S
    assert S % 512 == 0 and D % 1024 == 0
    tm_big = min(S, 1024)
    tm = 512
    tq_da, tq_mla, tk_mla = 256, 512, 256

    mod = _ada_mod(c, w_ada, b_ada)

    d = (positions - positions[:, :1]).astype(F32)
    d_col = d.reshape(T, 1)
    d_row = d.reshape(B, 1, S)
    tabs = _rope_tables(positions.astype(F32).reshape(T, 1))

    sl = jnp.asarray(ALIBI_SLOPES, F32) * LOG2E
    s1, s2, s3 = _split3(sl)
    qaug = jnp.stack([s1, s2, s3, s1, s2, s3], axis=-1)
    qaug = jnp.pad(qaug, ((0, 0), (0, LANES - 6))).reshape(DA_HEADS, 1, LANES)
    sl2 = jnp.broadcast_to((2.0 * sl)[:, None, None], (DA_HEADS, 1, LANES))

    da_scale = DA_QK ** -0.5 * LOG2E
    mla_scale = (MLA_NOPE + MLA_ROPE) ** -0.5 * LOG2E
    pad_r = lambda g: jnp.pad(g, (0, LANES - MLA_ROPE)).reshape(1, LANES)

    x2 = x.reshape(T, D)
    for l in range(L):
        mod_l = mod[l].reshape(B, 1, 6 * D)
        wi = w_in[l]
        w_in_p = jnp.concatenate(
            [wi[:, 2304:2816], wi[:, 3136:3648], wi[:, 3648:4160], wi[:, 0:768], wi[:, 768:1536],
             wi[:, 1536:2304], wi[:, 2816:3072], wi[:, 3072:3136],
             jnp.zeros((D, IN_PAD - C_KR - MLA_ROPE), wi.dtype)], axis=1).astype(BF16)
        z = _norm_matmul(x2, mod_l, 0, w_in_p, S, tm_big, 1536)

        lam_init = 0.8 - 0.6 * math.exp(-0.3 * l)
        gq = (jnp.tile(da_q_gain[l], 2) * da_scale).reshape(1, LANES)
        gk = jnp.tile(da_k_gain[l], 2).reshape(1, LANES)
        q_da, k_da = _da_prep(z, d_col, gq, gk, qaug, B, S, tm)
        lam_vecs = jnp.stack([da_lq1[l], da_lk1[l], da_lq2[l], da_lk2[l]])
        out_a = _da_attention(q_da, k_da, z, d_col, d_row, sl2, lam_vecs,
                              da_head_gain[l].reshape(DA_HEADS, 1, DA_V), B, S, tq_da, lam_init)

        wq = mla_w_uq[l].reshape(MLA_Q_RANK, MLA_HEADS, MLA_NOPE + MLA_ROPE)
        wqn = wq[:, :, :MLA_NOPE].reshape(MLA_Q_RANK, MLA_WIDTH).astype(BF16)
        wqr = jnp.pad(wq[:, :, MLA_NOPE:], ((0, 0), (0, 0), (0, LANES - MLA_ROPE))
                      ).reshape(MLA_Q_RANK, MLA_HEADS * LANES).astype(BF16)
        wkv = mla_w_ukv[l].reshape(MLA_KV_RANK, MLA_HEADS, MLA_NOPE + MLA_V)
        wkn = wkv[:, :, :MLA_NOPE].reshape(MLA_KV_RANK, MLA_WIDTH).astype(BF16)
        wv = wkv[:, :, MLA_NOPE:].reshape(MLA_KV_RANK, MLA_WIDTH).astype(BF16)
        qg, kg = mla_q_gain[l], mla_k_gain[l]
        q_m, k_m, v_m = _mla_prep(
            z, mla_q_a_gain[l].reshape(1, -1), mla_kv_a_gain[l].reshape(1, -1), wqn, wqr, wkn, wv,
            (qg[:MLA_NOPE] * mla_scale).reshape(1, LANES), pad_r(qg[MLA_NOPE:] * mla_scale),
            kg[:MLA_NOPE].reshape(1, LANES), pad_r(kg[MLA_NOPE:]), tabs, B, S, tm)
        out_b = _mla_attention(q_m, k_m, v_m, B, S, tq_mla, tk_mla)

        sg_bias = jnp.repeat(sg_b[l].T, SG_CH, axis=1)
        out_c = _sg(z, sg_v_gain[l].reshape(1, SG_WIDTH), sg_w[l], sg_bias, tm)

        x2 = _out_proj(out_a, out_b, out_c, w_out[l].astype(BF16), x2, mod_l, S, tm_big, 1024)
        x2 = _ffn(x2, mod_l, ffn_w_up[l].astype(BF16), ffn_conv_w[l], ffn_conv_b[l].reshape(1, -1),
                  ffn_w_down[l].astype(BF16), S, tm, 512)
    return x2.reshape(B, S, D)
```

```python
import functools
import math

import jax
import jax.numpy as jnp
from jax import lax
from jax.experimental import pallas as pl
from jax.experimental.pallas import tpu as pltpu

CHUNK = 64
EPS = 1e-6
DA_HEADS = 6
DA_QK = 64
DA_V = 128
MLA_HEADS = 6
MLA_Q_RANK = 512
MLA_KV_RANK = 256
MLA_NOPE = 128
MLA_ROPE = 64
MLA_V = 128
ROPE_THETA = 10000.0
SG_GROUPS = 4
SG_CH = 128
SG_LEN = 128
CONV_W = 3
DA_WIDTH = DA_HEADS * DA_V
MLA_WIDTH = MLA_HEADS * MLA_V
SG_WIDTH = SG_GROUPS * SG_CH
ALIBI_SLOPES = tuple(2.0 ** (-8.0 * (h + 1) / DA_HEADS) for h in range(DA_HEADS))

LOG2E = 1.4426950408889634
LANES = 128
QK_PAD = 256
NEG_BIG = -1e30
MAX_SHIFT = 100.0
BOUND_SLACK = 1.02
KV_UNROLL = 4
HALO = 16
NORM_ROWS = 256
VMEM_LIMIT = 56 * 1024 * 1024

C_QA, C_SGU, C_SGV, C_DAQ, C_DAK, C_DAV, C_KVA, C_KR, IN_PAD = (
    0, 512, 1024, 1536, 2304, 3072, 3840, 4096, 4608)

BF16 = jnp.bfloat16
F32 = jnp.float32


def _cp(*sem):
    return pltpu.CompilerParams(dimension_semantics=sem, vmem_limit_bytes=VMEM_LIMIT)


def _dot(a, b):
    return jnp.dot(a, b, preferred_element_type=F32)


def _dot_nt(a, b):
    return lax.dot_general(a, b, (((1,), (1,)), ((), ())), preferred_element_type=F32)


def _rms_scale(x, n):
    return lax.rsqrt(jnp.sum(x * x, axis=-1, keepdims=True) * (1.0 / n) + EPS)


def _ada_kernel(c_ref, w_ref, b_ref, o_ref):
    c = c_ref[...]
    cond = c / (1.0 + jnp.exp(-c))
    o_ref[0] = jnp.dot(cond, w_ref[0], preferred_element_type=F32,
                       precision=lax.Precision.HIGHEST) + b_ref[0]


def _ada_mod(c, w_ada, b_ada):
    L, D, N = w_ada.shape
    B = c.shape[0]
    tn = 1024
    return pl.pallas_call(
        _ada_kernel,
        grid=(L, N // tn),
        in_specs=[pl.BlockSpec((B, D), lambda l, j: (0, 0)),
                  pl.BlockSpec((1, D, tn), lambda l, j: (l, 0, j)),
                  pl.BlockSpec((1, 1, tn), lambda l, j: (l, 0, j))],
        out_specs=pl.BlockSpec((1, B, tn), lambda l, j: (l, 0, j)),
        out_shape=jax.ShapeDtypeStruct((L, B, N), F32),
        compiler_params=_cp("parallel", "parallel"),
        name="ada_mod",
    )(c, w_ada, b_ada.reshape(L, 1, N))


def _rope_table_kernel(pos_ref, inv_ref, c_ref, s1_ref, s2_ref):
    ang = pos_ref[...] * inv_ref[...]
    lane = lax.broadcasted_iota(jnp.int32, ang.shape, 1)
    half = MLA_ROPE // 2
    cos = jnp.cos(ang)
    sin = jnp.sin(ang)
    c_ref[...] = jnp.where(lane < MLA_ROPE, cos, 0.0)
    s1_ref[...] = jnp.where(lane < half, -sin, 0.0)
    s2_ref[...] = jnp.where((lane >= half) & (lane < MLA_ROPE), sin, 0.0)


def _rope_tables(pos_col):
    T = pos_col.shape[0]
    half = MLA_ROPE // 2
    inv = ROPE_THETA ** (-jnp.arange(half, dtype=F32) / half)
    inv_row = jnp.concatenate([inv, inv, jnp.zeros((LANES - MLA_ROPE,), F32)]).reshape(1, LANES)
    tm = min(T, 2048)
    sds = jax.ShapeDtypeStruct((T, LANES), F32)
    spec = pl.BlockSpec((tm, LANES), lambda i: (i, 0))
    return pl.pallas_call(
        _rope_table_kernel,
        grid=(T // tm,),
        in_specs=[pl.BlockSpec((tm, 1), lambda i: (i, 0)),
                  pl.BlockSpec((1, LANES), lambda i: (0, 0))],
        out_specs=[spec, spec, spec],
        out_shape=[sds, sds, sds],
        compiler_params=_cp("parallel"),
        name="rope_tables",
    )(pos_col, inv_row)


def _norm_matmul_kernel(x_ref, sh_ref, sc_ref, w_ref, o_ref, h_scr):
    @pl.when(pl.program_id(1) == 0)
    def _():
        for r in range(0, x_ref.shape[0], NORM_ROWS):
            x = x_ref[r:r + NORM_ROWS, :]
            h = x * _rms_scale(x, x.shape[-1]) * (1.0 + sc_ref[0]) + sh_ref[0]
            h_scr[r:r + NORM_ROWS, :] = h.astype(BF16)

    o_ref[...] = _dot(h_scr[...], w_ref[...]).astype(o_ref.dtype)


def _norm_matmul(x2, mod_l, which, w, S, tm, tn):
    T, D = x2.shape
    N = w.shape[1]
    tpb = S // tm
    return pl.pallas_call(
        _norm_matmul_kernel,
        grid=(T // tm, N // tn),
        in_specs=[pl.BlockSpec((tm, D), lambda i, j: (i, 0)),
                  pl.BlockSpec((1, 1, D), lambda i, j: (i // tpb, 0, which)),
                  pl.BlockSpec((1, 1, D), lambda i, j: (i // tpb, 0, which + 1)),
                  pl.BlockSpec((D, tn), lambda i, j: (0, j))],
        out_specs=pl.BlockSpec((tm, tn), lambda i, j: (i, j)),
        out_shape=jax.ShapeDtypeStruct((T, N), BF16),
        scratch_shapes=[pltpu.VMEM((tm, D), BF16)],
        compiler_params=_cp("parallel", "arbitrary"),
        name="norm_matmul",
    )(x2, mod_l, mod_l, w)


def _da_prep_kernel(zq_ref, zk_ref, zv_ref, d_ref, gq_ref, gk_ref, qc_ref, kc_ref, q_ref, k_ref, vt_ref):
    tm = zq_ref.shape[0]
    lane = lax.broadcasted_iota(jnp.int32, (1, LANES), 1)
    lo = lane < DA_QK

    def qk_norm(z, g):
        sq = z * z
        ss_all = jnp.sum(sq, axis=-1, keepdims=True)
        ss_lo = jnp.sum(jnp.where(lo, sq, 0.0), axis=-1, keepdims=True)
        inv_lo = lax.rsqrt(ss_lo * (1.0 / DA_QK) + EPS)
        inv_hi = lax.rsqrt((ss_all - ss_lo) * (1.0 / DA_QK) + EPS)
        return z * jnp.where(lo, inv_lo, inv_hi) * g

    d = d_ref[...]
    d_hi = jnp.floor(d * (1.0 / LANES)) * LANES
    d_lo = d - d_hi
    k_pos = jnp.where(lane < 3, d_hi, jnp.where(lane < 6, d_lo, 0.0))
    q_pos = jnp.where((lane >= 6) & (lane < 9), -d_hi, jnp.where((lane >= 9) & (lane < 12), -d_lo, 0.0))
    for h in range(DA_HEADS):
        sl = slice(h * LANES, (h + 1) * LANES)
        qn = qk_norm(zq_ref[:, sl].astype(F32), gq_ref[...])
        kn = qk_norm(zk_ref[:, sl].astype(F32), gk_ref[...])
        qa = (q_pos + qc_ref[h]).astype(BF16)
        q_ref[0, h, 0, :, :LANES] = jnp.where(lo, qn, 0.0).astype(BF16)
        q_ref[0, h, 0, :, LANES:] = qa
        q_ref[0, h, 1, :, :LANES] = jnp.where(lo, 0.0, qn).astype(BF16)
        q_ref[0, h, 1, :, LANES:] = qa
        k_ref[0, h, :, :LANES] = kn.astype(BF16)
        k_ref[0, h, :, LANES:] = (k_pos + kc_ref[h]).astype(BF16)
        vt_ref[0, h] = zv_ref[:, sl].astype(F32).T.astype(BF16)


def _da_prep(z, d_col, gq, gk, qc, kc, B, S, tm):
    T = z.shape[0]
    tpb = S // tm
    H = DA_HEADS
    return pl.pallas_call(
        _da_prep_kernel,
        grid=(T // tm,),
        in_specs=[pl.BlockSpec((tm, DA_WIDTH), lambda i: (i, C_DAQ // DA_WIDTH)),
                  pl.BlockSpec((tm, DA_WIDTH), lambda i: (i, C_DAK // DA_WIDTH)),
                  pl.BlockSpec((tm, DA_WIDTH), lambda i: (i, C_DAV // DA_WIDTH)),
                  pl.BlockSpec((tm, 1), lambda i: (i, 0)),
                  pl.BlockSpec((1, LANES), lambda i: (0, 0)),
                  pl.BlockSpec((1, LANES), lambda i: (0, 0)),
                  pl.BlockSpec((H, 1, LANES), lambda i: (0, 0, 0)),
                  pl.BlockSpec((H, 1, LANES), lambda i: (0, 0, 0))],
        out_specs=[pl.BlockSpec((1, H, 2, tm, QK_PAD), lambda i: (i // tpb, 0, 0, i % tpb, 0)),
                   pl.BlockSpec((1, H, tm, QK_PAD), lambda i: (i // tpb, 0, i % tpb, 0)),
                   pl.BlockSpec((1, H, DA_V, tm), lambda i: (i // tpb, 0, 0, i % tpb))],
        out_shape=[jax.ShapeDtypeStruct((B, H, 2, S, QK_PAD), BF16),
                   jax.ShapeDtypeStruct((B, H, S, QK_PAD), BF16),
                   jax.ShapeDtypeStruct((B, H, DA_V, S), BF16)],
        compiler_params=_cp("parallel"),
        name="da_prep",
    )(z, z, z, d_col, gq, gk, qc, kc)


def _mla_prep_kernel(zqa_ref, zkva_ref, zkr_ref, gqa_ref, gkva_ref, wqn_ref, wqr_ref, wkn_ref, wv_ref,
                     gqn_ref, gqr_ref, gkn_ref, gkr_ref, qb_ref, kb_ref, c_ref, s1_ref, s2_ref,
                     q_ref, k_ref, v_ref):
    half = MLA_ROPE // 2
    width = MLA_NOPE + MLA_ROPE
    cos, sin1, sin2 = c_ref[...], s1_ref[...], s2_ref[...]

    def rope(t):
        return (t * cos + pltpu.roll(t, LANES - half, 1) * sin1 + pltpu.roll(t, half, 1) * sin2)

    zqa = zqa_ref[...].astype(F32)
    qa = (zqa * _rms_scale(zqa, MLA_Q_RANK) * gqa_ref[...]).astype(BF16)
    zkva = zkva_ref[...].astype(F32)
    kva = (zkva * _rms_scale(zkva, MLA_KV_RANK) * gkva_ref[...]).astype(BF16)
    qn_all = _dot(qa, wqn_ref[...])
    qr_all = _dot(qa, wqr_ref[...])
    kn_all = _dot(kva, wkn_ref[...])
    v_all = _dot(kva, wv_ref[...])
    kr = zkr_ref[...].astype(F32)
    kr_ss = jnp.sum(kr * kr, axis=-1, keepdims=True)
    kr_rot = rope(kr * gkr_ref[...])
    for h in range(MLA_HEADS):
        sl = slice(h * LANES, (h + 1) * LANES)
        qn, qr, kn = qn_all[:, sl], qr_all[:, sl], kn_all[:, sl]
        q_inv = lax.rsqrt((jnp.sum(qn * qn, axis=-1, keepdims=True)
                           + jnp.sum(qr * qr, axis=-1, keepdims=True)) * (1.0 / width) + EPS)
        k_inv = lax.rsqrt((jnp.sum(kn * kn, axis=-1, keepdims=True) + kr_ss) * (1.0 / width) + EPS)
        q_ref[0, h, :, :LANES] = (qn * q_inv * gqn_ref[...]).astype(BF16)
        q_ref[0, h, :, LANES:] = (rope(qr * q_inv * gqr_ref[...]) + qb_ref[...]).astype(BF16)
        k_ref[0, h, :, :LANES] = (kn * k_inv * gkn_ref[...]).astype(BF16)
        k_ref[0, h, :, LANES:] = (kr_rot * k_inv + kb_ref[...]).astype(BF16)
        v_ref[0, h] = v_all[:, sl].T.astype(BF16)


def _mla_prep(z, gqa, gkva, wqn, wqr, wkn, wv, gqn, gqr, gkn, gkr, qb, kb, tabs, B, S, tm):
    T = z.shape[0]
    tpb = S // tm
    H = MLA_HEADS
    full = lambda a: pl.BlockSpec(a.shape, lambda i: (0,) * a.ndim)
    row = pl.BlockSpec((tm, LANES), lambda i: (i, 0))
    return pl.pallas_call(
        _mla_prep_kernel,
        grid=(T // tm,),
        in_specs=[pl.BlockSpec((tm, MLA_Q_RANK), lambda i: (i, C_QA // MLA_Q_RANK)),
                  pl.BlockSpec((tm, MLA_KV_RANK), lambda i: (i, C_KVA // MLA_KV_RANK)),
                  pl.BlockSpec((tm, LANES), lambda i: (i, C_KR // LANES)),
                  full(gqa), full(gkva), full(wqn), full(wqr), full(wkn), full(wv),
                  full(gqn), full(gqr), full(gkn), full(gkr), full(qb), full(kb), row, row, row],
        out_specs=[pl.BlockSpec((1, H, tm, QK_PAD), lambda i: (i // tpb, 0, i % tpb, 0)),
                   pl.BlockSpec((1, H, tm, QK_PAD), lambda i: (i // tpb, 0, i % tpb, 0)),
                   pl.BlockSpec((1, H, MLA_V, tm), lambda i: (i // tpb, 0, 0, i % tpb))],
        out_shape=[jax.ShapeDtypeStruct((B, H, S, QK_PAD), BF16),
                   jax.ShapeDtypeStruct((B, H, S, QK_PAD), BF16),
                   jax.ShapeDtypeStruct((B, H, MLA_V, S), BF16)],
        compiler_params=_cp("parallel"),
        name="mla_prep",
    )(z, z, z, gqa, gkva, wqn, wqr, wkn, wv, gqn, gqr, gkn, gkr, qb, kb, *tabs)


def _gelu_tanh(x):
    return 0.5 * x * (1.0 + jnp.tanh(0.7978845608028654 * (x + 0.044715 * (x * x * x))))


def _sg_kernel(zu_ref, zv_ref, g_ref, w_ref, b_ref, o_ref):
    tm = zu_ref.shape[0]
    r = lax.broadcasted_iota(jnp.int32, (SG_LEN, SG_LEN), 0)
    c = lax.broadcasted_iota(jnp.int32, (SG_LEN, SG_LEN), 1)
    causal = c <= r
    for g in range(SG_GROUPS):
        sl = slice(g * SG_CH, (g + 1) * SG_CH)
        wt = jnp.where(causal, w_ref[g], 0.0).astype(BF16)
        v = _gelu_tanh(zv_ref[:, sl].astype(F32))
        v = (v * _rms_scale(v, SG_CH) * g_ref[:, sl]).astype(BF16)
        u = _gelu_tanh(zu_ref[:, sl].astype(F32))
        for n in range(tm // SG_LEN):
            rows = slice(n * SG_LEN, (n + 1) * SG_LEN)
            s = _dot(wt, v[rows, :]) + b_ref[:, sl]
            o_ref[rows, sl] = (u[rows, :] * s).astype(BF16)


def _sg(z, gain, w, bias, tm):
    T = z.shape[0]
    return pl.pallas_call(
        _sg_kernel,
        grid=(T // tm,),
        in_specs=[pl.BlockSpec((tm, SG_WIDTH), lambda i: (i, C_SGU // SG_WIDTH)),
                  pl.BlockSpec((tm, SG_WIDTH), lambda i: (i, C_SGV // SG_WIDTH)),
                  pl.BlockSpec((1, SG_WIDTH), lambda i: (0, 0)),
                  pl.BlockSpec((SG_GROUPS, SG_LEN, SG_LEN), lambda i: (0, 0, 0)),
                  pl.BlockSpec((SG_LEN, SG_WIDTH), lambda i: (0, 0))],
        out_specs=pl.BlockSpec((tm, SG_WIDTH), lambda i: (i, 0)),
        out_shape=jax.ShapeDtypeStruct((T, SG_WIDTH), BF16),
        compiler_params=_cp("parallel"),
        name="spatial_gating",
    )(z, z, gain, w, bias)


def _flash_t(q, k_ref, vt_ref, n_off, tq, fix, online):
    n = q.shape[0]

    def step(j, carry, fix=None):
        m, l, acc = carry
        start = pl.multiple_of(j * tq, tq)
        s = _dot_nt(k_ref[0, 0, pl.ds(start, tq), :], q)
        if fix is not None:
            s = fix(s)
        if online:
            m_new = jnp.maximum(m, jnp.max(s, axis=0, keepdims=True))
            alpha = jnp.exp2(m - m_new)
            p = jnp.exp2(s - m_new)
            l = alpha * l
            acc = alpha * acc
        else:
            m_new = m
            p = jnp.exp2(s)
        l = l + jnp.sum(p, axis=0, keepdims=True)
        acc = acc + _dot(vt_ref[0, 0, :, pl.ds(start, tq)], p.astype(BF16))
        return m_new, l, acc

    def group(g, carry):
        for u in range(KV_UNROLL):
            carry = step(g * KV_UNROLL + u, carry)
        return carry

    carry = (jnp.full((1, n), NEG_BIG, F32), jnp.zeros((1, n), F32), jnp.zeros((DA_V, n), F32))
    n_grp = n_off // KV_UNROLL
    carry = lax.fori_loop(0, n_grp, group, carry)
    carry = lax.fori_loop(n_grp * KV_UNROLL, n_off, step, carry)
    _, l, acc = step(n_off, carry, fix)
    return acc / l


def _chunk_visible(tq, n):
    r = lax.broadcasted_iota(jnp.int32, (tq, n), 0)
    c = lax.broadcasted_iota(jnp.int32, (tq, n), 1)
    c = jnp.where(c >= tq, c - tq, c)
    return lax.shift_right_logical(r, 6) <= lax.shift_right_logical(c, 6)


def _da_attn_kernel(q_ref, k_ref, vt_ref, dk_ref, dq_ref, sl2_ref, lam_ref, hg_ref, o_ref, *,
                    tq, lam_init, online):
    qi = pl.program_id(2)
    q = q_ref[0, 0].reshape(2 * tq, QK_PAD)
    dq = dq_ref[0]
    dq2 = jnp.concatenate([dq, dq], axis=1)
    corr = sl2_ref[0, :, :1] * jnp.maximum(dk_ref[...] - dq2, 0.0)
    visible = _chunk_visible(tq, 2 * tq)
    ot = _flash_t(q, k_ref, vt_ref, qi, tq, lambda s: jnp.where(visible, s - corr, NEG_BIG), online)
    lv = lam_ref[...]
    lam = (jnp.exp(jnp.sum(lv[0:1] * lv[1:2], axis=-1, keepdims=True))
           - jnp.exp(jnp.sum(lv[2:3] * lv[3:4], axis=-1, keepdims=True)) + lam_init)
    od = ot[:, :tq] - lam * ot[:, tq:]
    inv = lax.rsqrt(jnp.sum(od * od, axis=0, keepdims=True) * (1.0 / DA_V) + EPS)
    od = od * inv * (1.0 - lam_init)
    o_ref[...] = (od.T * hg_ref[0]).astype(o_ref.dtype)


def _da_attention(q, k, vt, d_col, d_row, sl2, lam_vecs, hg, B, S, tq, lam_init, online):
    H = DA_HEADS
    nq = S // tq
    return pl.pallas_call(
        functools.partial(_da_attn_kernel, tq=tq, lam_init=lam_init, online=online),
        grid=(B, H, nq),
        in_specs=[pl.BlockSpec((1, 1, 2, tq, QK_PAD), lambda b, h, i: (b, h, 0, i, 0)),
                  pl.BlockSpec((1, 1, S, QK_PAD), lambda b, h, i: (b, h, 0, 0)),
                  pl.BlockSpec((1, 1, DA_V, S), lambda b, h, i: (b, h, 0, 0)),
                  pl.BlockSpec((tq, 1), lambda b, h, i: (b * nq + i, 0)),
                  pl.BlockSpec((1, 1, tq), lambda b, h, i: (b, 0, i)),
                  pl.BlockSpec((1, 1, LANES), lambda b, h, i: (h, 0, 0)),
                  pl.BlockSpec((4, DA_QK), lambda b, h, i: (0, 0)),
                  pl.BlockSpec((1, 1, DA_V), lambda b, h, i: (h, 0, 0))],
        out_specs=pl.BlockSpec((tq, DA_V), lambda b, h, i: (b * nq + i, h)),
        out_shape=jax.ShapeDtypeStruct((B * S, DA_WIDTH), BF16),
        compiler_params=_cp("parallel", "parallel", "parallel"),
        name="da_attention",
    )(q, k, vt, d_col, d_row, sl2, lam_vecs, hg)


def _mla_attn_kernel(q_ref, k_ref, vt_ref, o_ref, *, tq, online):
    qi = pl.program_id(2)
    visible = _chunk_visible(tq, tq)
    ot = _flash_t(q_ref[0, 0], k_ref, vt_ref, qi, tq, lambda s: jnp.where(visible, s, NEG_BIG), online)
    o_ref[...] = ot.T.astype(o_ref.dtype)


def _mla_attention(q, k, vt, B, S, tq, online):
    H = MLA_HEADS
    nq = S // tq
    return pl.pallas_call(
        functools.partial(_mla_attn_kernel, tq=tq, online=online),
        grid=(B, H, nq),
        in_specs=[pl.BlockSpec((1, 1, tq, QK_PAD), lambda b, h, i: (b, h, i, 0)),
                  pl.BlockSpec((1, 1, S, QK_PAD), lambda b, h, i: (b, h, 0, 0)),
                  pl.BlockSpec((1, 1, MLA_V, S), lambda b, h, i: (b, h, 0, 0))],
        out_specs=pl.BlockSpec((tq, MLA_V), lambda b, h, i: (b * nq + i, h)),
        out_shape=jax.ShapeDtypeStruct((B * S, MLA_WIDTH), BF16),
        compiler_params=_cp("parallel", "parallel", "parallel"),
        name="mla_attention",
    )(q, k, vt)


def _out_proj_kernel(a_ref, b_ref, c_ref, wa_ref, wb_ref, wc_ref, x_ref, g_ref, o_ref):
    mix = _dot(a_ref[...], wa_ref[...]) + _dot(b_ref[...], wb_ref[...]) + _dot(c_ref[...], wc_ref[...])
    o_ref[...] = x_ref[...] + g_ref[0] * mix


def _out_proj(a, b, c, w, x2, mod_l, S, tm, tn):
    T, D = x2.shape
    tpb = S // tm
    nb = D // tn
    return pl.pallas_call(
        _out_proj_kernel,
        grid=(T // tm, nb),
        in_specs=[pl.BlockSpec((tm, DA_WIDTH), lambda i, j: (i, 0)),
                  pl.BlockSpec((tm, MLA_WIDTH), lambda i, j: (i, 0)),
                  pl.BlockSpec((tm, SG_WIDTH), lambda i, j: (i, 0)),
                  pl.BlockSpec((DA_WIDTH, tn), lambda i, j: (0, j)),
                  pl.BlockSpec((MLA_WIDTH, tn), lambda i, j: (1, j)),
                  pl.BlockSpec((SG_WIDTH, tn), lambda i, j: ((DA_WIDTH + MLA_WIDTH) // SG_WIDTH, j)),
                  pl.BlockSpec((tm, tn), lambda i, j: (i, j)),
                  pl.BlockSpec((1, 1, tn), lambda i, j: (i // tpb, 0, 2 * nb + j))],
        out_specs=pl.BlockSpec((tm, tn), lambda i, j: (i, j)),
        out_shape=jax.ShapeDtypeStruct((T, D), F32),
        compiler_params=_cp("parallel", "parallel"),
        name="out_proj",
    )(a, b, c, w, w, w, x2, mod_l)


def _ffn_kernel(x_ref, xh_ref, sh_ref, sc_ref, g_ref, wg_ref, wv_ref, cwg_ref, cwv_ref, cbg_ref, cbv_ref,
                wd_ref, o_ref, h_scr, acc_scr, *, tpb):
    i = pl.program_id(0)
    j = pl.program_id(1)

    @pl.when(j == 0)
    def _():
        def modnorm(x):
            return x * _rms_scale(x, x.shape[-1]) * (1.0 + sc_ref[0]) + sh_ref[0]

        for r in range(0, x_ref.shape[0], NORM_ROWS):
            h_scr[HALO + r:HALO + r + NORM_ROWS, :] = modnorm(x_ref[r:r + NORM_ROWS, :]).astype(BF16)
        keep = jnp.where(i % tpb == 0, 0.0, 1.0)
        h_scr[:HALO, :] = (modnorm(xh_ref[...]) * keep).astype(BF16)
        acc_scr[...] = jnp.zeros_like(acc_scr)

    h = h_scr[...]

    def conv(a, cw_ref, cb_ref):
        y = (cb_ref[...] + a * cw_ref[2:3, :] + pltpu.roll(a, 1, 0) * cw_ref[1:2, :]
             + pltpu.roll(a, 2, 0) * cw_ref[0:1, :])
        return y[HALO:, :]

    yg = conv(_dot(h, wg_ref[...]), cwg_ref, cbg_ref)
    yv = conv(_dot(h, wv_ref[...]), cwv_ref, cbv_ref)
    act = (yg / (1.0 + jnp.exp(-yg)) * yv).astype(BF16)
    acc_scr[...] += _dot(act, wd_ref[...])

    @pl.when(j == pl.num_programs(1) - 1)
    def _():
        o_ref[...] = x_ref[...] + g_ref[0] * acc_scr[...]


def _ffn(x2, mod_l, w_up, conv_w, conv_b, w_down, S, tm, tf):
    T, D = x2.shape
    F = w_down.shape[0]
    nf = F // tf
    tpb = S // tm
    hb = tm // HALO
    return pl.pallas_call(
        functools.partial(_ffn_kernel, tpb=tpb),
        grid=(T // tm, nf),
        in_specs=[pl.BlockSpec((tm, D), lambda i, j: (i, 0)),
                  pl.BlockSpec((HALO, D), lambda i, j: (jnp.maximum(i * hb - 1, 0), 0)),
                  pl.BlockSpec((1, 1, D), lambda i, j: (i // tpb, 0, 3)),
                  pl.BlockSpec((1, 1, D), lambda i, j: (i // tpb, 0, 4)),
                  pl.BlockSpec((1, 1, D), lambda i, j: (i // tpb, 0, 5)),
                  pl.BlockSpec((D, tf), lambda i, j: (0, j)),
                  pl.BlockSpec((D, tf), lambda i, j: (0, nf + j)),
                  pl.BlockSpec((CONV_W, tf), lambda i, j: (0, j)),
                  pl.BlockSpec((CONV_W, tf), lambda i, j: (0, nf + j)),
                  pl.BlockSpec((1, tf), lambda i, j: (0, j)),
                  pl.BlockSpec((1, tf), lambda i, j: (0, nf + j)),
                  pl.BlockSpec((tf, D), lambda i, j: (j, 0))],
        out_specs=pl.BlockSpec((tm, D), lambda i, j: (i, 0)),
        out_shape=jax.ShapeDtypeStruct((T, D), F32),
        scratch_shapes=[pltpu.VMEM((HALO + tm, D), BF16), pltpu.VMEM((tm, D), F32)],
        compiler_params=_cp("parallel", "arbitrary"),
        name="conv_ffn",
    )(x2, x2, mod_l, mod_l, mod_l, w_up, w_up, conv_w, conv_w, conv_b, conv_b, w_down)


def _split3(v):
    a = v.astype(BF16).astype(F32)
    b = (v - a).astype(BF16).astype(F32)
    c = (v - a - b).astype(BF16).astype(F32)
    return a, b, c


def kernel(x, c, positions, w_ada, b_ada, w_in, da_q_gain, da_k_gain, da_lq1, da_lk1, da_lq2, da_lk2, da_head_gain, mla_q_a_gain, mla_w_uq, mla_kv_a_gain, mla_w_ukv, mla_q_gain, mla_k_gain, sg_v_gain, sg_w, sg_b, w_out, ffn_w_up, ffn_conv_w, ffn_conv_b, ffn_w_down):
    B, S, D = x.shape
    L = w_ada.shape[0]
    T = B * S
    assert S % 512 == 0 and D % 1024 == 0
    tm_big = min(S, 1024)
    tm = 512
    tq = 512

    mod = _ada_mod(c, w_ada, b_ada)

    d = (positions - positions[:, :1]).astype(F32)
    d_col = d.reshape(T, 1)
    d_row = d.reshape(B, 1, S)
    tabs = _rope_tables(positions.astype(F32).reshape(T, 1))

    H = DA_HEADS
    sl = jnp.asarray(ALIBI_SLOPES, F32) * LOG2E
    slope6 = jnp.tile(jnp.stack(_split3(sl), axis=-1), (1, 2))
    zeros = lambda n: jnp.zeros((H, n), F32)
    kc = jnp.concatenate([zeros(6), slope6, jnp.ones((H, 3), F32), zeros(LANES - 15)], axis=-1)
    kc = kc.reshape(H, 1, LANES)
    sl2 = jnp.broadcast_to((2.0 * sl)[:, None, None], (H, 1, LANES))

    da_scale = DA_QK ** -0.5 * LOG2E
    mla_width = MLA_NOPE + MLA_ROPE
    mla_scale = mla_width ** -0.5 * LOG2E
    pad_r = lambda g: jnp.pad(g, (0, LANES - MLA_ROPE)).reshape(1, LANES)
    shift_lanes = lambda v: jnp.pad(v, (MLA_ROPE, LANES - MLA_ROPE - 3)).reshape(1, LANES)
    kb_mla = shift_lanes(jnp.ones((3,), F32))
    gmax = lambda g: jnp.max(jnp.abs(g))

    x2 = x.reshape(T, D)
    for l in range(L):
        mod_l = mod[l].reshape(B, 1, 6 * D)
        wi = w_in[l]
        w_in_p = jnp.concatenate(
            [wi[:, 2304:2816], wi[:, 3136:3648], wi[:, 3648:4160], wi[:, 0:768], wi[:, 768:1536],
             wi[:, 1536:2304], wi[:, 2816:3072], wi[:, 3072:3136],
             jnp.zeros((D, IN_PAD - C_KR - MLA_ROPE), wi.dtype)], axis=1).astype(BF16)
        z = _norm_matmul(x2, mod_l, 0, w_in_p, S, tm_big, 1536)

        lam_init = 0.8 - 0.6 * math.exp(-0.3 * l)
        gq = (jnp.tile(da_q_gain[l], 2) * da_scale).reshape(1, LANES)
        gk = jnp.tile(da_k_gain[l], 2).reshape(1, LANES)
        bound = BOUND_SLACK * DA_QK * da_scale * gmax(da_q_gain[l]) * gmax(da_k_gain[l])
        nb = jnp.broadcast_to(-jnp.stack(_split3(bound)), (H, 3))
        qc = jnp.concatenate([slope6, zeros(6), nb, zeros(LANES - 15)], axis=-1).reshape(H, 1, LANES)
        q_da, k_da, vt_da = _da_prep(z, d_col, gq, gk, qc, kc, B, S, tm)
        lam_vecs = jnp.stack([da_lq1[l], da_lk1[l], da_lq2[l], da_lk2[l]])
        da_args = (q_da, k_da, vt_da, d_col, d_row, sl2, lam_vecs, da_head_gain[l].reshape(H, 1, DA_V))
        out_a = lax.cond(
            2.0 * bound <= MAX_SHIFT,
            lambda a: _da_attention(*a, B, S, tq, lam_init, online=False),
            lambda a: _da_attention(*a, B, S, tq, lam_init, online=True), da_args)

        wq = mla_w_uq[l].reshape(MLA_Q_RANK, MLA_HEADS, MLA_NOPE + MLA_ROPE)
        wqn = wq[:, :, :MLA_NOPE].reshape(MLA_Q_RANK, MLA_WIDTH).astype(BF16)
        wqr = jnp.pad(wq[:, :, MLA_NOPE:], ((0, 0), (0, 0), (0, LANES - MLA_ROPE))
                      ).reshape(MLA_Q_RANK, MLA_HEADS * LANES).astype(BF16)
        wkv = mla_w_ukv[l].reshape(MLA_KV_RANK, MLA_HEADS, MLA_NOPE + MLA_V)
        wkn = wkv[:, :, :MLA_NOPE].reshape(MLA_KV_RANK, MLA_WIDTH).astype(BF16)
        wv = wkv[:, :, MLA_NOPE:].reshape(MLA_KV_RANK, MLA_WIDTH).astype(BF16)
        qg, kg = mla_q_gain[l], mla_k_gain[l]
        bound_m = BOUND_SLACK * mla_width * mla_scale * gmax(qg) * gmax(kg)
        q_m, k_m, v_m = _mla_prep(
            z, mla_q_a_gain[l].reshape(1, -1), mla_kv_a_gain[l].reshape(1, -1), wqn, wqr, wkn, wv,
            (qg[:MLA_NOPE] * mla_scale).reshape(1, LANES), pad_r(qg[MLA_NOPE:] * mla_scale),
            kg[:MLA_NOPE].reshape(1, LANES), pad_r(kg[MLA_NOPE:]),
            shift_lanes(-jnp.stack(_split3(bound_m))), kb_mla, tabs, B, S, tm)
        out_b = lax.cond(
            2.0 * bound_m <= MAX_SHIFT,
            lambda a: _mla_attention(*a, B, S, tq, online=False),
            lambda a: _mla_attention(*a, B, S, tq, online=True), (q_m, k_m, v_m))

        sg_bias = jnp.repeat(sg_b[l].T, SG_CH, axis=1)
        out_c = _sg(z, sg_v_gain[l].reshape(1, SG_WIDTH), sg_w[l], sg_bias, tm)

        x2 = _out_proj(out_a, out_b, out_c, w_out[l].astype(BF16), x2, mod_l, S, tm_big, 1024)
        x2 = _ffn(x2, mod_l, ffn_w_up[l].astype(BF16), ffn_conv_w[l], ffn_conv_b[l].reshape(1, -1),
                  ffn_w_down[l].astype(BF16), S, tm, 512)
    return x2.reshape(B, S, D)
```

```python
import functools
import math

import jax
import jax.numpy as jnp
from jax import lax
from jax.experimental import pallas as pl
from jax.experimental.pallas import tpu as pltpu

CHUNK = 64
EPS = 1e-6
DA_HEADS = 6
DA_QK = 64
DA_V = 128
MLA_HEADS = 6
MLA_Q_RANK = 512
MLA_KV_RANK = 256
MLA_NOPE = 128
MLA_ROPE = 64
MLA_V = 128
ROPE_THETA = 10000.0
SG_GROUPS = 4
SG_CH = 128
SG_LEN = 128
CONV_W = 3
DA_WIDTH = DA_HEADS * DA_V
MLA_WIDTH = MLA_HEADS * MLA_V
SG_WIDTH = SG_GROUPS * SG_CH
ALIBI_SLOPES = tuple(2.0 ** (-8.0 * (h + 1) / DA_HEADS) for h in range(DA_HEADS))

LOG2E = 1.4426950408889634
LANES = 128
QK_PAD = 256
NEG_BIG = -1e30
MAX_SHIFT = 100.0
BOUND_SLACK = 1.02
KV_UNROLL = 4
KEY_TILE = 512
HALO = 16
NORM_ROWS = 256
VMEM_LIMIT = 56 * 1024 * 1024

C_QA, C_SGU, C_SGV, C_DAQ, C_DAK, C_DAV, C_KVA, C_KR, IN_PAD = (
    0, 512, 1024, 1536, 2304, 3072, 3840, 4096, 4608)

BF16 = jnp.bfloat16
F32 = jnp.float32


def _cp(*sem):
    return pltpu.CompilerParams(dimension_semantics=sem, vmem_limit_bytes=VMEM_LIMIT)


def _dot(a, b):
    return jnp.dot(a, b, preferred_element_type=F32)


def _dot_nt(a, b):
    return lax.dot_general(a, b, (((1,), (1,)), ((), ())), preferred_element_type=F32)


def _rms_scale(x, n):
    return lax.rsqrt(jnp.sum(x * x, axis=-1, keepdims=True) * (1.0 / n) + EPS)


def _ada_kernel(c_ref, w_ref, b_ref, o_ref):
    c = c_ref[...]
    cond = c / (1.0 + jnp.exp(-c))
    o_ref[0] = jnp.dot(cond, w_ref[0], preferred_element_type=F32,
                       precision=lax.Precision.HIGHEST) + b_ref[0]


def _ada_mod(c, w_ada, b_ada):
    L, D, N = w_ada.shape
    B = c.shape[0]
    tn = 1024
    return pl.pallas_call(
        _ada_kernel,
        grid=(L, N // tn),
        in_specs=[pl.BlockSpec((B, D), lambda l, j: (0, 0)),
                  pl.BlockSpec((1, D, tn), lambda l, j: (l, 0, j)),
                  pl.BlockSpec((1, 1, tn), lambda l, j: (l, 0, j))],
        out_specs=pl.BlockSpec((1, B, tn), lambda l, j: (l, 0, j)),
        out_shape=jax.ShapeDtypeStruct((L, B, N), F32),
        compiler_params=_cp("parallel", "parallel"),
        name="ada_mod",
    )(c, w_ada, b_ada.reshape(L, 1, N))


def _rope_table_kernel(pos_ref, inv_ref, c_ref, s1_ref, s2_ref):
    ang = pos_ref[...] * inv_ref[...]
    lane = lax.broadcasted_iota(jnp.int32, ang.shape, 1)
    half = MLA_ROPE // 2
    cos = jnp.cos(ang)
    sin = jnp.sin(ang)
    c_ref[...] = jnp.where(lane < MLA_ROPE, cos, 0.0)
    s1_ref[...] = jnp.where(lane < half, -sin, 0.0)
    s2_ref[...] = jnp.where((lane >= half) & (lane < MLA_ROPE), sin, 0.0)


def _rope_tables(pos_col):
    T = pos_col.shape[0]
    half = MLA_ROPE // 2
    inv = ROPE_THETA ** (-jnp.arange(half, dtype=F32) / half)
    inv_row = jnp.concatenate([inv, inv, jnp.zeros((LANES - MLA_ROPE,), F32)]).reshape(1, LANES)
    tm = min(T, 2048)
    sds = jax.ShapeDtypeStruct((T, LANES), F32)
    spec = pl.BlockSpec((tm, LANES), lambda i: (i, 0))
    return pl.pallas_call(
        _rope_table_kernel,
        grid=(T // tm,),
        in_specs=[pl.BlockSpec((tm, 1), lambda i: (i, 0)),
                  pl.BlockSpec((1, LANES), lambda i: (0, 0))],
        out_specs=[spec, spec, spec],
        out_shape=[sds, sds, sds],
        compiler_params=_cp("parallel"),
        name="rope_tables",
    )(pos_col, inv_row)


def _norm_matmul_kernel(x_ref, sh_ref, sc_ref, w_ref, o_ref, h_scr):
    @pl.when(pl.program_id(1) == 0)
    def _():
        for r in range(0, x_ref.shape[0], NORM_ROWS):
            x = x_ref[r:r + NORM_ROWS, :]
            h = x * _rms_scale(x, x.shape[-1]) * (1.0 + sc_ref[0]) + sh_ref[0]
            h_scr[r:r + NORM_ROWS, :] = h.astype(BF16)

    o_ref[...] = _dot(h_scr[...], w_ref[...]).astype(o_ref.dtype)


def _norm_matmul(x2, mod_l, which, w, S, tm, tn):
    T, D = x2.shape
    N = w.shape[1]
    tpb = S // tm
    return pl.pallas_call(
        _norm_matmul_kernel,
        grid=(T // tm, N // tn),
        in_specs=[pl.BlockSpec((tm, D), lambda i, j: (i, 0)),
                  pl.BlockSpec((1, 1, D), lambda i, j: (i // tpb, 0, which)),
                  pl.BlockSpec((1, 1, D), lambda i, j: (i // tpb, 0, which + 1)),
                  pl.BlockSpec((D, tn), lambda i, j: (0, j))],
        out_specs=pl.BlockSpec((tm, tn), lambda i, j: (i, j)),
        out_shape=jax.ShapeDtypeStruct((T, N), BF16),
        scratch_shapes=[pltpu.VMEM((tm, D), BF16)],
        compiler_params=_cp("parallel", "arbitrary"),
        name="norm_matmul",
    )(x2, mod_l, mod_l, w)


def _da_prep_kernel(zq_ref, zk_ref, zv_ref, d_ref, gq_ref, gk_ref, qc_ref, kc_ref, q_ref, k_ref, vt_ref):
    tm = zq_ref.shape[0]
    lane = lax.broadcasted_iota(jnp.int32, (1, LANES), 1)
    lo = lane < DA_QK

    def qk_norm(z, g):
        sq = z * z
        ss_all = jnp.sum(sq, axis=-1, keepdims=True)
        ss_lo = jnp.sum(jnp.where(lo, sq, 0.0), axis=-1, keepdims=True)
        inv_lo = lax.rsqrt(ss_lo * (1.0 / DA_QK) + EPS)
        inv_hi = lax.rsqrt((ss_all - ss_lo) * (1.0 / DA_QK) + EPS)
        return z * jnp.where(lo, inv_lo, inv_hi) * g

    d = d_ref[...]
    d_hi = jnp.floor(d * (1.0 / LANES)) * LANES
    d_lo = d - d_hi
    k_pos = jnp.where(lane < 3, d_hi, jnp.where(lane < 6, d_lo, 0.0))
    q_pos = jnp.where((lane >= 6) & (lane < 9), -d_hi, jnp.where((lane >= 9) & (lane < 12), -d_lo, 0.0))
    for h in range(DA_HEADS):
        sl = slice(h * LANES, (h + 1) * LANES)
        qn = qk_norm(zq_ref[:, sl].astype(F32), gq_ref[...])
        kn = qk_norm(zk_ref[:, sl].astype(F32), gk_ref[...])
        qa = (q_pos + qc_ref[h]).astype(BF16)
        q_ref[0, h, 0, :, :LANES] = jnp.where(lo, qn, 0.0).astype(BF16)
        q_ref[0, h, 0, :, LANES:] = qa
        q_ref[0, h, 1, :, :LANES] = jnp.where(lo, 0.0, qn).astype(BF16)
        q_ref[0, h, 1, :, LANES:] = qa
        k_ref[0, h, :, :LANES] = kn.astype(BF16)
        k_ref[0, h, :, LANES:] = (k_pos + kc_ref[h]).astype(BF16)
        vt_ref[0, h] = zv_ref[:, sl].astype(F32).T.astype(BF16)


def _da_prep(z, d_col, gq, gk, qc, kc, B, S, tm):
    T = z.shape[0]
    tpb = S // tm
    H = DA_HEADS
    return pl.pallas_call(
        _da_prep_kernel,
        grid=(T // tm,),
        in_specs=[pl.BlockSpec((tm, DA_WIDTH), lambda i: (i, C_DAQ // DA_WIDTH)),
                  pl.BlockSpec((tm, DA_WIDTH), lambda i: (i, C_DAK // DA_WIDTH)),
                  pl.BlockSpec((tm, DA_WIDTH), lambda i: (i, C_DAV // DA_WIDTH)),
                  pl.BlockSpec((tm, 1), lambda i: (i, 0)),
                  pl.BlockSpec((1, LANES), lambda i: (0, 0)),
                  pl.BlockSpec((1, LANES), lambda i: (0, 0)),
                  pl.BlockSpec((H, 1, LANES), lambda i: (0, 0, 0)),
                  pl.BlockSpec((H, 1, LANES), lambda i: (0, 0, 0))],
        out_specs=[pl.BlockSpec((1, H, 2, tm, QK_PAD), lambda i: (i // tpb, 0, 0, i % tpb, 0)),
                   pl.BlockSpec((1, H, tm, QK_PAD), lambda i: (i // tpb, 0, i % tpb, 0)),
                   pl.BlockSpec((1, H, DA_V, tm), lambda i: (i // tpb, 0, 0, i % tpb))],
        out_shape=[jax.ShapeDtypeStruct((B, H, 2, S, QK_PAD), BF16),
                   jax.ShapeDtypeStruct((B, H, S, QK_PAD), BF16),
                   jax.ShapeDtypeStruct((B, H, DA_V, S), BF16)],
        compiler_params=_cp("parallel"),
        name="da_prep",
    )(z, z, z, d_col, gq, gk, qc, kc)


def _mla_prep_kernel(zqa_ref, zkva_ref, zkr_ref, gqa_ref, gkva_ref, wqn_ref, wqr_ref, wkn_ref, wv_ref,
                     gqn_ref, gqr_ref, gkn_ref, gkr_ref, qb_ref, kb_ref, c_ref, s1_ref, s2_ref,
                     q_ref, k_ref, v_ref):
    half = MLA_ROPE // 2
    width = MLA_NOPE + MLA_ROPE
    cos, sin1, sin2 = c_ref[...], s1_ref[...], s2_ref[...]

    def rope(t):
        return (t * cos + pltpu.roll(t, LANES - half, 1) * sin1 + pltpu.roll(t, half, 1) * sin2)

    zqa = zqa_ref[...].astype(F32)
    qa = (zqa * _rms_scale(zqa, MLA_Q_RANK) * gqa_ref[...]).astype(BF16)
    zkva = zkva_ref[...].astype(F32)
    kva = (zkva * _rms_scale(zkva, MLA_KV_RANK) * gkva_ref[...]).astype(BF16)
    qn_all = _dot(qa, wqn_ref[...])
    qr_all = _dot(qa, wqr_ref[...])
    kn_all = _dot(kva, wkn_ref[...])
    v_all = _dot(kva, wv_ref[...])
    kr = zkr_ref[...].astype(F32)
    kr_ss = jnp.sum(kr * kr, axis=-1, keepdims=True)
    kr_rot = rope(kr * gkr_ref[...])
    for h in range(MLA_HEADS):
        sl = slice(h * LANES, (h + 1) * LANES)
        qn, qr, kn = qn_all[:, sl], qr_all[:, sl], kn_all[:, sl]
        q_inv = lax.rsqrt((jnp.sum(qn * qn, axis=-1, keepdims=True)
                           + jnp.sum(qr * qr, axis=-1, keepdims=True)) * (1.0 / width) + EPS)
        k_inv = lax.rsqrt((jnp.sum(kn * kn, axis=-1, keepdims=True) + kr_ss) * (1.0 / width) + EPS)
        q_ref[0, h, :, :LANES] = (qn * q_inv * gqn_ref[...]).astype(BF16)
        q_ref[0, h, :, LANES:] = (rope(qr * q_inv * gqr_ref[...]) + qb_ref[...]).astype(BF16)
        k_ref[0, h, :, :LANES] = (kn * k_inv * gkn_ref[...]).astype(BF16)
        k_ref[0, h, :, LANES:] = (kr_rot * k_inv + kb_ref[...]).astype(BF16)
        v_ref[0, h] = v_all[:, sl].T.astype(BF16)


def _mla_prep(z, gqa, gkva, wqn, wqr, wkn, wv, gqn, gqr, gkn, gkr, qb, kb, tabs, B, S, tm):
    T = z.shape[0]
    tpb = S // tm
    H = MLA_HEADS
    full = lambda a: pl.BlockSpec(a.shape, lambda i: (0,) * a.ndim)
    row = pl.BlockSpec((tm, LANES), lambda i: (i, 0))
    return pl.pallas_call(
        _mla_prep_kernel,
        grid=(T // tm,),
        in_specs=[pl.BlockSpec((tm, MLA_Q_RANK), lambda i: (i, C_QA // MLA_Q_RANK)),
                  pl.BlockSpec((tm, MLA_KV_RANK), lambda i: (i, C_KVA // MLA_KV_RANK)),
                  pl.BlockSpec((tm, LANES), lambda i: (i, C_KR // LANES)),
                  full(gqa), full(gkva), full(wqn), full(wqr), full(wkn), full(wv),
                  full(gqn), full(gqr), full(gkn), full(gkr), full(qb), full(kb), row, row, row],
        out_specs=[pl.BlockSpec((1, H, tm, QK_PAD), lambda i: (i // tpb, 0, i % tpb, 0)),
                   pl.BlockSpec((1, H, tm, QK_PAD), lambda i: (i // tpb, 0, i % tpb, 0)),
                   pl.BlockSpec((1, H, MLA_V, tm), lambda i: (i // tpb, 0, 0, i % tpb))],
        out_shape=[jax.ShapeDtypeStruct((B, H, S, QK_PAD), BF16),
                   jax.ShapeDtypeStruct((B, H, S, QK_PAD), BF16),
                   jax.ShapeDtypeStruct((B, H, MLA_V, S), BF16)],
        compiler_params=_cp("parallel"),
        name="mla_prep",
    )(z, z, z, gqa, gkva, wqn, wqr, wkn, wv, gqn, gqr, gkn, gkr, qb, kb, *tabs)


def _gelu_tanh(x):
    return 0.5 * x * (1.0 + jnp.tanh(0.7978845608028654 * (x + 0.044715 * (x * x * x))))


def _sg_kernel(zu_ref, zv_ref, g_ref, w_ref, b_ref, o_ref):
    tm = zu_ref.shape[0]
    r = lax.broadcasted_iota(jnp.int32, (SG_LEN, SG_LEN), 0)
    c = lax.broadcasted_iota(jnp.int32, (SG_LEN, SG_LEN), 1)
    causal = c <= r
    for g in range(SG_GROUPS):
        sl = slice(g * SG_CH, (g + 1) * SG_CH)
        wt = jnp.where(causal, w_ref[g], 0.0).astype(BF16)
        v = _gelu_tanh(zv_ref[:, sl].astype(F32))
        v = (v * _rms_scale(v, SG_CH) * g_ref[:, sl]).astype(BF16)
        u = _gelu_tanh(zu_ref[:, sl].astype(F32))
        for n in range(tm // SG_LEN):
            rows = slice(n * SG_LEN, (n + 1) * SG_LEN)
            s = _dot(wt, v[rows, :]) + b_ref[:, sl]
            o_ref[rows, sl] = (u[rows, :] * s).astype(BF16)


def _sg(z, gain, w, bias, tm):
    T = z.shape[0]
    return pl.pallas_call(
        _sg_kernel,
        grid=(T // tm,),
        in_specs=[pl.BlockSpec((tm, SG_WIDTH), lambda i: (i, C_SGU // SG_WIDTH)),
                  pl.BlockSpec((tm, SG_WIDTH), lambda i: (i, C_SGV // SG_WIDTH)),
                  pl.BlockSpec((1, SG_WIDTH), lambda i: (0, 0)),
                  pl.BlockSpec((SG_GROUPS, SG_LEN, SG_LEN), lambda i: (0, 0, 0)),
                  pl.BlockSpec((SG_LEN, SG_WIDTH), lambda i: (0, 0))],
        out_specs=pl.BlockSpec((tm, SG_WIDTH), lambda i: (i, 0)),
        out_shape=jax.ShapeDtypeStruct((T, SG_WIDTH), BF16),
        compiler_params=_cp("parallel"),
        name="spatial_gating",
    )(z, z, gain, w, bias)


def _flash_t(q, k_ref, vt_ref, n_off, diag_fixes, online):
    n = q.shape[0]

    def step(j, state, fix=None):
        m, l, acc = state
        start = pl.multiple_of(j * KEY_TILE, KEY_TILE)
        s = _dot_nt(k_ref[0, 0, pl.ds(start, KEY_TILE), :], q)
        if fix is not None:
            s = fix(s)
        if online:
            m_new = jnp.maximum(m, jnp.max(s, axis=0, keepdims=True))
            alpha = jnp.exp2(m - m_new)
            p = jnp.exp2(s - m_new)
            l = alpha * l
            acc = alpha * acc
        else:
            m_new = m
            p = jnp.exp2(s)
        l = l + jnp.sum(p, axis=0, keepdims=True)
        acc = acc + _dot(vt_ref[0, 0, :, pl.ds(start, KEY_TILE)], p.astype(BF16))
        return m_new, l, acc

    def group(g, state):
        for u in range(KV_UNROLL):
            state = step(g * KV_UNROLL + u, state)
        return state

    state = (jnp.full((1, n), NEG_BIG, F32), jnp.zeros((1, n), F32), jnp.zeros((DA_V, n), F32))
    n_grp = n_off // KV_UNROLL
    state = lax.fori_loop(0, n_grp, group, state)
    state = lax.fori_loop(n_grp * KV_UNROLL, n_off, step, state)
    for t, fix in enumerate(diag_fixes):
        state = step(n_off + t, state, fix)
    _, l, acc = state
    return acc / l


def _chunk_visible(n, tq, key_off):
    r = lax.broadcasted_iota(jnp.int32, (KEY_TILE, n), 0) + key_off
    c = lax.broadcasted_iota(jnp.int32, (KEY_TILE, n), 1)
    if n > tq:
        c = jnp.where(c >= tq, c - tq, c)
    shift = CHUNK.bit_length() - 1
    return lax.shift_right_logical(r, shift) <= lax.shift_right_logical(c, shift)


def _da_attn_kernel(q_ref, k_ref, vt_ref, dk_ref, dq_ref, sl2_ref, lam_ref, hg_ref, o_ref, *,
                    tq, lam_init, online):
    qi = pl.program_id(2)
    q = q_ref[0, 0].reshape(2 * tq, QK_PAD)
    dq = dq_ref[0]
    dq2 = jnp.concatenate([dq, dq], axis=1)
    corr = sl2_ref[0, :, :1] * jnp.maximum(dk_ref[...] - dq2, 0.0)
    visible = _chunk_visible(2 * tq, tq, 0)
    ot = _flash_t(q, k_ref, vt_ref, qi, [lambda s: jnp.where(visible, s - corr, NEG_BIG)], online)
    lv = lam_ref[...]
    lam = (jnp.exp(jnp.sum(lv[0:1] * lv[1:2], axis=-1, keepdims=True))
           - jnp.exp(jnp.sum(lv[2:3] * lv[3:4], axis=-1, keepdims=True)) + lam_init)
    od = ot[:, :tq] - lam * ot[:, tq:]
    inv = lax.rsqrt(jnp.sum(od * od, axis=0, keepdims=True) * (1.0 / DA_V) + EPS)
    od = od * inv * (1.0 - lam_init)
    o_ref[...] = (od.T * hg_ref[0]).astype(o_ref.dtype)


def _da_attention(q, k, vt, d_col, d_row, sl2, lam_vecs, hg, B, S, tq, lam_init, online):
    H = DA_HEADS
    nq = S // tq
    assert tq == KEY_TILE
    return pl.pallas_call(
        functools.partial(_da_attn_kernel, tq=tq, lam_init=lam_init, online=online),
        grid=(B, H, nq),
        in_specs=[pl.BlockSpec((1, 1, 2, tq, QK_PAD), lambda b, h, i: (b, h, 0, i, 0)),
                  pl.BlockSpec((1, 1, S, QK_PAD), lambda b, h, i: (b, h, 0, 0)),
                  pl.BlockSpec((1, 1, DA_V, S), lambda b, h, i: (b, h, 0, 0)),
                  pl.BlockSpec((tq, 1), lambda b, h, i: (b * nq + i, 0)),
                  pl.BlockSpec((1, 1, tq), lambda b, h, i: (b, 0, i)),
                  pl.BlockSpec((1, 1, LANES), lambda b, h, i: (h, 0, 0)),
                  pl.BlockSpec((4, DA_QK), lambda b, h, i: (0, 0)),
                  pl.BlockSpec((1, 1, DA_V), lambda b, h, i: (h, 0, 0))],
        out_specs=pl.BlockSpec((tq, DA_V), lambda b, h, i: (b * nq + i, h)),
        out_shape=jax.ShapeDtypeStruct((B * S, DA_WIDTH), BF16),
        compiler_params=_cp("parallel", "parallel", "parallel"),
        name="da_attention",
    )(q, k, vt, d_col, d_row, sl2, lam_vecs, hg)


def _mla_attn_kernel(q_ref, k_ref, vt_ref, o_ref, *, tq, online):
    qi = pl.program_id(2)
    tiles = tq // KEY_TILE
    fixes = []
    for t in range(tiles):
        visible = _chunk_visible(tq, tq, t * KEY_TILE)
        fixes.append(lambda s, vis=visible: jnp.where(vis, s, NEG_BIG))
    ot = _flash_t(q_ref[0, 0], k_ref, vt_ref, qi * tiles, fixes, online)
    o_ref[...] = ot.T.astype(o_ref.dtype)


def _mla_attention(q, k, vt, B, S, tq, online):
    H = MLA_HEADS
    nq = S // tq
    return pl.pallas_call(
        functools.partial(_mla_attn_kernel, tq=tq, online=online),
        grid=(B, H, nq),
        in_specs=[pl.BlockSpec((1, 1, tq, QK_PAD), lambda b, h, i: (b, h, i, 0)),
                  pl.BlockSpec((1, 1, S, QK_PAD), lambda b, h, i: (b, h, 0, 0)),
                  pl.BlockSpec((1, 1, MLA_V, S), lambda b, h, i: (b, h, 0, 0))],
        out_specs=pl.BlockSpec((tq, MLA_V), lambda b, h, i: (b * nq + i, h)),
        out_shape=jax.ShapeDtypeStruct((B * S, MLA_WIDTH), BF16),
        compiler_params=_cp("parallel", "parallel", "parallel"),
        name="mla_attention",
    )(q, k, vt)


def _out_proj_kernel(a_ref, b_ref, c_ref, wa_ref, wb_ref, wc_ref, x_ref, g_ref, o_ref):
    mix = _dot(a_ref[...], wa_ref[...]) + _dot(b_ref[...], wb_ref[...]) + _dot(c_ref[...], wc_ref[...])
    o_ref[...] = x_ref[...] + g_ref[0] * mix


def _out_proj(a, b, c, w, x2, mod_l, S, tm, tn):
    T, D = x2.shape
    tpb = S // tm
    nb = D // tn
    return pl.pallas_call(
        _out_proj_kernel,
        grid=(T // tm, nb),
        in_specs=[pl.BlockSpec((tm, DA_WIDTH), lambda i, j: (i, 0)),
                  pl.BlockSpec((tm, MLA_WIDTH), lambda i, j: (i, 0)),
                  pl.BlockSpec((tm, SG_WIDTH), lambda i, j: (i, 0)),
                  pl.BlockSpec((DA_WIDTH, tn), lambda i, j: (0, j)),
                  pl.BlockSpec((MLA_WIDTH, tn), lambda i, j: (1, j)),
                  pl.BlockSpec((SG_WIDTH, tn), lambda i, j: ((DA_WIDTH + MLA_WIDTH) // SG_WIDTH, j)),
                  pl.BlockSpec((tm, tn), lambda i, j: (i, j)),
                  pl.BlockSpec((1, 1, tn), lambda i, j: (i // tpb, 0, 2 * nb + j))],
        out_specs=pl.BlockSpec((tm, tn), lambda i, j: (i, j)),
        out_shape=jax.ShapeDtypeStruct((T, D), F32),
        compiler_params=_cp("parallel", "parallel"),
        name="out_proj",
    )(a, b, c, w, w, w, x2, mod_l)


def _ffn_kernel(x_ref, xh_ref, sh_ref, sc_ref, g_ref, wg_ref, wv_ref, cwg_ref, cwv_ref, cbg_ref, cbv_ref,
                wd_ref, o_ref, h_scr, *, tpb):
    i = pl.program_id(0)
    j = pl.program_id(1)

    @pl.when(j == 0)
    def _():
        def modnorm(x):
            return x * _rms_scale(x, x.shape[-1]) * (1.0 + sc_ref[0]) + sh_ref[0]

        for r in range(0, x_ref.shape[0], NORM_ROWS):
            h_scr[HALO + r:HALO + r + NORM_ROWS, :] = modnorm(x_ref[r:r + NORM_ROWS, :]).astype(BF16)
        keep = jnp.where(i % tpb == 0, 0.0, 1.0)
        h_scr[:HALO, :] = (modnorm(xh_ref[...]) * keep).astype(BF16)
        o_ref[...] = jnp.zeros_like(o_ref)

    h = h_scr[...]

    def conv(a, cw_ref, cb_ref):
        y = (cb_ref[...] + a * cw_ref[2:3, :] + pltpu.roll(a, 1, 0) * cw_ref[1:2, :]
             + pltpu.roll(a, 2, 0) * cw_ref[0:1, :])
        return y[HALO:, :]

    yg = conv(_dot(h, wg_ref[...]), cwg_ref, cbg_ref)
    yv = conv(_dot(h, wv_ref[...]), cwv_ref, cbv_ref)
    act = (yg / (1.0 + jnp.exp(-yg)) * yv).astype(BF16)
    o_ref[...] += _dot(act, wd_ref[...])

    @pl.when(j == pl.num_programs(1) - 1)
    def _():
        o_ref[...] = x_ref[...] + g_ref[0] * o_ref[...]


def _ffn(x2, mod_l, w_up, conv_w, conv_b, w_down, S, tm, tf):
    T, D = x2.shape
    F = w_down.shape[0]
    nf = F // tf
    tpb = S // tm
    hb = tm // HALO
    return pl.pallas_call(
        functools.partial(_ffn_kernel, tpb=tpb),
        grid=(T // tm, nf),
        in_specs=[pl.BlockSpec((tm, D), lambda i, j: (i, 0)),
                  pl.BlockSpec((HALO, D), lambda i, j: (jnp.maximum(i * hb - 1, 0), 0)),
                  pl.BlockSpec((1, 1, D), lambda i, j: (i // tpb, 0, 3)),
                  pl.BlockSpec((1, 1, D), lambda i, j: (i // tpb, 0, 4)),
                  pl.BlockSpec((1, 1, D), lambda i, j: (i // tpb, 0, 5)),
                  pl.BlockSpec((D, tf), lambda i, j: (0, j)),
                  pl.BlockSpec((D, tf), lambda i, j: (0, nf + j)),
                  pl.BlockSpec((CONV_W, tf), lambda i, j: (0, j)),
                  pl.BlockSpec((CONV_W, tf), lambda i, j: (0, nf + j)),
                  pl.BlockSpec((1, tf), lambda i, j: (0, j)),
                  pl.BlockSpec((1, tf), lambda i, j: (0, nf + j)),
                  pl.BlockSpec((tf, D), lambda i, j: (j, 0))],
        out_specs=pl.BlockSpec((tm, D), lambda i, j: (i, 0), pipeline_mode=pl.Buffered(1)),
        out_shape=jax.ShapeDtypeStruct((T, D), F32),
        scratch_shapes=[pltpu.VMEM((HALO + tm, D), BF16)],
        compiler_params=_cp("parallel", "arbitrary"),
        name="conv_ffn",
    )(x2, x2, mod_l, mod_l, mod_l, w_up, w_up, conv_w, conv_w, conv_b, conv_b, w_down)


def _split3(v):
    a = v.astype(BF16).astype(F32)
    b = (v - a).astype(BF16).astype(F32)
    c = (v - a - b).astype(BF16).astype(F32)
    return a, b, c


def kernel(x, c, positions, w_ada, b_ada, w_in, da_q_gain, da_k_gain, da_lq1, da_lk1, da_lq2, da_lk2, da_head_gain, mla_q_a_gain, mla_w_uq, mla_kv_a_gain, mla_w_ukv, mla_q_gain, mla_k_gain, sg_v_gain, sg_w, sg_b, w_out, ffn_w_up, ffn_conv_w, ffn_conv_b, ffn_w_down):
    B, S, D = x.shape
    L = w_ada.shape[0]
    T = B * S
    assert S % 512 == 0 and D % 1024 == 0
    tm_big = min(S, 1024)
    tm = 512
    tq_mla = min(S, 1024)

    mod = _ada_mod(c, w_ada, b_ada)

    d = (positions - positions[:, :1]).astype(F32)
    d_col = d.reshape(T, 1)
    d_row = d.reshape(B, 1, S)
    tabs = _rope_tables(positions.astype(F32).reshape(T, 1))

    H = DA_HEADS
    sl = jnp.asarray(ALIBI_SLOPES, F32) * LOG2E
    slope6 = jnp.tile(jnp.stack(_split3(sl), axis=-1), (1, 2))
    zeros = lambda n: jnp.zeros((H, n), F32)
    kc = jnp.concatenate([zeros(6), slope6, jnp.ones((H, 3), F32), zeros(LANES - 15)], axis=-1)
    kc = kc.reshape(H, 1, LANES)
    sl2 = jnp.broadcast_to((2.0 * sl)[:, None, None], (H, 1, LANES))

    da_scale = DA_QK ** -0.5 * LOG2E
    mla_width = MLA_NOPE + MLA_ROPE
    mla_scale = mla_width ** -0.5 * LOG2E
    pad_r = lambda g: jnp.pad(g, (0, LANES - MLA_ROPE)).reshape(1, LANES)
    shift_lanes = lambda v: jnp.pad(v, (MLA_ROPE, LANES - MLA_ROPE - 3)).reshape(1, LANES)
    kb_mla = shift_lanes(jnp.ones((3,), F32))
    gmax = lambda g: jnp.max(jnp.abs(g))

    x2 = x.reshape(T, D)
    for l in range(L):
        mod_l = mod[l].reshape(B, 1, 6 * D)
        wi = w_in[l]
        w_in_p = jnp.concatenate(
            [wi[:, 2304:2816], wi[:, 3136:3648], wi[:, 3648:4160], wi[:, 0:768], wi[:, 768:1536],
             wi[:, 1536:2304], wi[:, 2816:3072], wi[:, 3072:3136],
             jnp.zeros((D, IN_PAD - C_KR - MLA_ROPE), wi.dtype)], axis=1).astype(BF16)
        z = _norm_matmul(x2, mod_l, 0, w_in_p, S, tm_big, 1536)

        lam_init = 0.8 - 0.6 * math.exp(-0.3 * l)
        gq = (jnp.tile(da_q_gain[l], 2) * da_scale).reshape(1, LANES)
        gk = jnp.tile(da_k_gain[l], 2).reshape(1, LANES)
        bound = BOUND_SLACK * DA_QK * da_scale * gmax(da_q_gain[l]) * gmax(da_k_gain[l])
        nb = jnp.broadcast_to(-jnp.stack(_split3(bound)), (H, 3))
        qc = jnp.concatenate([slope6, zeros(6), nb, zeros(LANES - 15)], axis=-1).reshape(H, 1, LANES)
        q_da, k_da, vt_da = _da_prep(z, d_col, gq, gk, qc, kc, B, S, tm)
        lam_vecs = jnp.stack([da_lq1[l], da_lk1[l], da_lq2[l], da_lk2[l]])
        da_args = (q_da, k_da, vt_da, d_col, d_row, sl2, lam_vecs, da_head_gain[l].reshape(H, 1, DA_V))
        out_a = lax.cond(
            2.0 * bound <= MAX_SHIFT,
            lambda a: _da_attention(*a, B, S, KEY_TILE, lam_init, online=False),
            lambda a: _da_attention(*a, B, S, KEY_TILE, lam_init, online=True), da_args)

        wq = mla_w_uq[l].reshape(MLA_Q_RANK, MLA_HEADS, MLA_NOPE + MLA_ROPE)
        wqn = wq[:, :, :MLA_NOPE].reshape(MLA_Q_RANK, MLA_WIDTH).astype(BF16)
        wqr = jnp.pad(wq[:, :, MLA_NOPE:], ((0, 0), (0, 0), (0, LANES - MLA_ROPE))
                      ).reshape(MLA_Q_RANK, MLA_HEADS * LANES).astype(BF16)
        wkv = mla_w_ukv[l].reshape(MLA_KV_RANK, MLA_HEADS, MLA_NOPE + MLA_V)
        wkn = wkv[:, :, :MLA_NOPE].reshape(MLA_KV_RANK, MLA_WIDTH).astype(BF16)
        wv = wkv[:, :, MLA_NOPE:].reshape(MLA_KV_RANK, MLA_WIDTH).astype(BF16)
        qg, kg = mla_q_gain[l], mla_k_gain[l]
        bound_m = BOUND_SLACK * mla_width * mla_scale * gmax(qg) * gmax(kg)
        q_m, k_m, v_m = _mla_prep(
            z, mla_q_a_gain[l].reshape(1, -1), mla_kv_a_gain[l].reshape(1, -1), wqn, wqr, wkn, wv,
            (qg[:MLA_NOPE] * mla_scale).reshape(1, LANES), pad_r(qg[MLA_NOPE:] * mla_scale),
            kg[:MLA_NOPE].reshape(1, LANES), pad_r(kg[MLA_NOPE:]),
            shift_lanes(-jnp.stack(_split3(bound_m))), kb_mla, tabs, B, S, tm)
        out_b = lax.cond(
            2.0 * bound_m <= MAX_SHIFT,
            lambda a: _mla_attention(*a, B, S, tq_mla, online=False),
            lambda a: _mla_attention(*a, B, S, tq_mla, online=True), (q_m, k_m, v_m))

        sg_bias = jnp.repeat(sg_b[l].T, SG_CH, axis=1)
        out_c = _sg(z, sg_v_gain[l].reshape(1, SG_WIDTH), sg_w[l], sg_bias, tm)

        x2 = _out_proj(out_a, out_b, out_c, w_out[l].astype(BF16), x2, mod_l, S, tm_big, 1024)
        x2 = _ffn(x2, mod_l, ffn_w_up[l].astype(BF16), ffn_conv_w[l], ffn_conv_b[l].reshape(1, -1),
                  ffn_w_down[l].astype(BF16), S, tm_big, 512)
    return x2.reshape(B, S, D)
```

```python
import functools
import math

import jax
import jax.numpy as jnp
from jax import lax
from jax.experimental import pallas as pl
from jax.experimental.pallas import tpu as pltpu

CHUNK = 64
EPS = 1e-6
DA_HEADS = 6
DA_QK = 64
DA_V = 128
MLA_HEADS = 6
MLA_Q_RANK = 512
MLA_KV_RANK = 256
MLA_NOPE = 128
MLA_ROPE = 64
MLA_V = 128
ROPE_THETA = 10000.0
SG_GROUPS = 4
SG_CH = 128
SG_LEN = 128
CONV_W = 3
DA_WIDTH = DA_HEADS * DA_V
MLA_WIDTH = MLA_HEADS * MLA_V
SG_WIDTH = SG_GROUPS * SG_CH
ALIBI_SLOPES = tuple(2.0 ** (-8.0 * (h + 1) / DA_HEADS) for h in range(DA_HEADS))

LOG2E = 1.4426950408889634
LANES = 128
QK_PAD = 256
NEG_BIG = -1e30
MAX_SHIFT = 100.0
BOUND_SLACK = 1.02
KV_UNROLL = 4
KEY_TILE = 512
HALO = 16
NORM_ROWS = 256
VMEM_LIMIT = 56 * 1024 * 1024

C_QA, C_SGU, C_SGV, C_DAQ, C_DAK, C_DAV, C_KVA, C_KR, IN_PAD = (
    0, 512, 1024, 1536, 2304, 3072, 3840, 4096, 4224)
IN_TILE = IN_PAD // 3

BF16 = jnp.bfloat16
F32 = jnp.float32


def _cp(*sem):
    return pltpu.CompilerParams(dimension_semantics=sem, vmem_limit_bytes=VMEM_LIMIT)


def _dot(a, b):
    return jnp.dot(a, b, preferred_element_type=F32)


def _dot_nt(a, b):
    return lax.dot_general(a, b, (((1,), (1,)), ((), ())), preferred_element_type=F32)


def _rms_scale(x, n):
    return lax.rsqrt(jnp.sum(x * x, axis=-1, keepdims=True) * (1.0 / n) + EPS)


def _same_group_matrix(group):
    r = lax.broadcasted_iota(jnp.int32, (LANES, LANES), 0)
    c = lax.broadcasted_iota(jnp.int32, (LANES, LANES), 1)
    shift = group.bit_length() - 1
    same = lax.shift_right_logical(r, shift) == lax.shift_right_logical(c, shift)
    return jnp.where(same, 1.0, 0.0).astype(BF16)


def _ada_kernel(c_ref, w_ref, b_ref, o_ref):
    c = c_ref[...]
    cond = c / (1.0 + jnp.exp(-c))
    o_ref[0] = jnp.dot(cond, w_ref[0], preferred_element_type=F32,
                       precision=lax.Precision.HIGHEST) + b_ref[0]


def _ada_mod(c, w_ada, b_ada):
    L, D, N = w_ada.shape
    B = c.shape[0]
    tn = 1024
    return pl.pallas_call(
        _ada_kernel,
        grid=(L, N // tn),
        in_specs=[pl.BlockSpec((B, D), lambda l, j: (0, 0)),
                  pl.BlockSpec((1, D, tn), lambda l, j: (l, 0, j)),
                  pl.BlockSpec((1, 1, tn), lambda l, j: (l, 0, j))],
        out_specs=pl.BlockSpec((1, B, tn), lambda l, j: (l, 0, j)),
        out_shape=jax.ShapeDtypeStruct((L, B, N), F32),
        compiler_params=_cp("parallel", "parallel"),
        name="ada_mod",
    )(c, w_ada, b_ada.reshape(L, 1, N))


def _rope_table_kernel(pos_ref, inv_ref, c_ref, s1_ref, s2_ref):
    ang = pos_ref[...] * inv_ref[...]
    lane = lax.broadcasted_iota(jnp.int32, ang.shape, 1)
    half = MLA_ROPE // 2
    cos = jnp.cos(ang)
    sin = jnp.sin(ang)
    c_ref[...] = jnp.where(lane < MLA_ROPE, cos, 0.0)
    s1_ref[...] = jnp.where(lane < half, -sin, 0.0)
    s2_ref[...] = jnp.where((lane >= half) & (lane < MLA_ROPE), sin, 0.0)


def _rope_tables(pos_col):
    T = pos_col.shape[0]
    half = MLA_ROPE // 2
    inv = ROPE_THETA ** (-jnp.arange(half, dtype=F32) / half)
    inv_row = jnp.concatenate([inv, inv, jnp.zeros((LANES - MLA_ROPE,), F32)]).reshape(1, LANES)
    tm = min(T, 2048)
    sds = jax.ShapeDtypeStruct((T, LANES), F32)
    spec = pl.BlockSpec((tm, LANES), lambda i: (i, 0))
    return pl.pallas_call(
        _rope_table_kernel,
        grid=(T // tm,),
        in_specs=[pl.BlockSpec((tm, 1), lambda i: (i, 0)),
                  pl.BlockSpec((1, LANES), lambda i: (0, 0))],
        out_specs=[spec, spec, spec],
        out_shape=[sds, sds, sds],
        compiler_params=_cp("parallel"),
        name="rope_tables",
    )(pos_col, inv_row)


def _norm_matmul_kernel(x_ref, sh_ref, sc_ref, w_ref, o_ref, h_scr):
    @pl.when(pl.program_id(1) == 0)
    def _():
        for r in range(0, x_ref.shape[0], NORM_ROWS):
            x = x_ref[r:r + NORM_ROWS, :]
            h = x * _rms_scale(x, x.shape[-1]) * (1.0 + sc_ref[0]) + sh_ref[0]
            h_scr[r:r + NORM_ROWS, :] = h.astype(BF16)

    o_ref[...] = _dot(h_scr[...], w_ref[...]).astype(o_ref.dtype)


def _norm_matmul(x2, mod_l, which, w, S, tm, tn):
    T, D = x2.shape
    N = w.shape[1]
    tpb = S // tm
    return pl.pallas_call(
        _norm_matmul_kernel,
        grid=(T // tm, N // tn),
        in_specs=[pl.BlockSpec((tm, D), lambda i, j: (i, 0)),
                  pl.BlockSpec((1, 1, D), lambda i, j: (i // tpb, 0, which)),
                  pl.BlockSpec((1, 1, D), lambda i, j: (i // tpb, 0, which + 1)),
                  pl.BlockSpec((D, tn), lambda i, j: (0, j))],
        out_specs=pl.BlockSpec((tm, tn), lambda i, j: (i, j)),
        out_shape=jax.ShapeDtypeStruct((T, N), BF16),
        scratch_shapes=[pltpu.VMEM((tm, D), BF16)],
        compiler_params=_cp("parallel", "arbitrary"),
        name="norm_matmul",
    )(x2, mod_l, mod_l, w)


def _da_prep_kernel(zq_ref, zk_ref, zv_ref, d_ref, gq_ref, gk_ref, qc_ref, kc_ref, q_ref, k_ref, vt_ref):
    lane = lax.broadcasted_iota(jnp.int32, (1, LANES), 1)
    lo = lane < DA_QK
    same_map = _same_group_matrix(DA_QK)

    def qk_norm(z, g):
        ss = _dot((z * z).astype(BF16), same_map)
        return z * lax.rsqrt(ss * (1.0 / DA_QK) + EPS) * g

    d = d_ref[...]
    d_hi = jnp.floor(d * (1.0 / LANES)) * LANES
    d_lo = d - d_hi
    k_pos = jnp.where(lane < 3, d_hi, jnp.where(lane < 6, d_lo, 0.0))
    q_pos = jnp.where((lane >= 6) & (lane < 9), -d_hi, jnp.where((lane >= 9) & (lane < 12), -d_lo, 0.0))
    for h in range(DA_HEADS):
        sl = slice(h * LANES, (h + 1) * LANES)
        qn = qk_norm(zq_ref[:, sl].astype(F32), gq_ref[...])
        kn = qk_norm(zk_ref[:, sl].astype(F32), gk_ref[...])
        qa = (q_pos + qc_ref[h]).astype(BF16)
        q_ref[0, h, 0, :, :LANES] = jnp.where(lo, qn, 0.0).astype(BF16)
        q_ref[0, h, 0, :, LANES:] = qa
        q_ref[0, h, 1, :, :LANES] = jnp.where(lo, 0.0, qn).astype(BF16)
        q_ref[0, h, 1, :, LANES:] = qa
        k_ref[0, h, :, :LANES] = kn.astype(BF16)
        k_ref[0, h, :, LANES:] = (k_pos + kc_ref[h]).astype(BF16)
        vt_ref[0, h] = zv_ref[:, sl].astype(F32).T.astype(BF16)


def _da_prep(z, d_col, gq, gk, qc, kc, B, S, tm):
    T = z.shape[0]
    tpb = S // tm
    H = DA_HEADS
    return pl.pallas_call(
        _da_prep_kernel,
        grid=(T // tm,),
        in_specs=[pl.BlockSpec((tm, DA_WIDTH), lambda i: (i, C_DAQ // DA_WIDTH)),
                  pl.BlockSpec((tm, DA_WIDTH), lambda i: (i, C_DAK // DA_WIDTH)),
                  pl.BlockSpec((tm, DA_WIDTH), lambda i: (i, C_DAV // DA_WIDTH)),
                  pl.BlockSpec((tm, 1), lambda i: (i, 0)),
                  pl.BlockSpec((1, LANES), lambda i: (0, 0)),
                  pl.BlockSpec((1, LANES), lambda i: (0, 0)),
                  pl.BlockSpec((H, 1, LANES), lambda i: (0, 0, 0)),
                  pl.BlockSpec((H, 1, LANES), lambda i: (0, 0, 0))],
        out_specs=[pl.BlockSpec((1, H, 2, tm, QK_PAD), lambda i: (i // tpb, 0, 0, i % tpb, 0)),
                   pl.BlockSpec((1, H, tm, QK_PAD), lambda i: (i // tpb, 0, i % tpb, 0)),
                   pl.BlockSpec((1, H, DA_V, tm), lambda i: (i // tpb, 0, 0, i % tpb))],
        out_shape=[jax.ShapeDtypeStruct((B, H, 2, S, QK_PAD), BF16),
                   jax.ShapeDtypeStruct((B, H, S, QK_PAD), BF16),
                   jax.ShapeDtypeStruct((B, H, DA_V, S), BF16)],
        compiler_params=_cp("parallel"),
        name="da_prep",
    )(z, z, z, d_col, gq, gk, qc, kc)


def _mla_prep_kernel(zqa_ref, zkva_ref, zkr_ref, gqa_ref, gkva_ref, wqn_ref, wqr_ref, wkn_ref, wv_ref,
                     gqn_ref, gqr_ref, gkn_ref, gkr_ref, qb_ref, kb_ref, c_ref, s1_ref, s2_ref,
                     q_ref, k_ref, v_ref):
    half = MLA_ROPE // 2
    width = MLA_NOPE + MLA_ROPE
    cos, sin1, sin2 = c_ref[...], s1_ref[...], s2_ref[...]

    def rope(t):
        return (t * cos + pltpu.roll(t, LANES - half, 1) * sin1 + pltpu.roll(t, half, 1) * sin2)

    zqa = zqa_ref[...].astype(F32)
    qa = (zqa * _rms_scale(zqa, MLA_Q_RANK) * gqa_ref[...]).astype(BF16)
    zkva = zkva_ref[...].astype(F32)
    kva = (zkva * _rms_scale(zkva, MLA_KV_RANK) * gkva_ref[...]).astype(BF16)
    qn_all = _dot(qa, wqn_ref[...])
    qr_all = _dot(qa, wqr_ref[...])
    kn_all = _dot(kva, wkn_ref[...])
    v_all = _dot(kva, wv_ref[...])
    kr = zkr_ref[...].astype(F32)
    kr_sq = kr * kr
    kr_rot = rope(kr * gkr_ref[...])
    ones = _same_group_matrix(LANES)
    for h in range(MLA_HEADS):
        sl = slice(h * LANES, (h + 1) * LANES)
        qn, qr, kn = qn_all[:, sl], qr_all[:, sl], kn_all[:, sl]
        q_inv = lax.rsqrt(_dot((qn * qn + qr * qr).astype(BF16), ones) * (1.0 / width) + EPS)
        k_inv = lax.rsqrt(_dot((kn * kn + kr_sq).astype(BF16), ones) * (1.0 / width) + EPS)
        q_ref[0, h, :, :LANES] = (qn * q_inv * gqn_ref[...]).astype(BF16)
        q_ref[0, h, :, LANES:] = (rope(qr * q_inv * gqr_ref[...]) + qb_ref[...]).astype(BF16)
        k_ref[0, h, :, :LANES] = (kn * k_inv * gkn_ref[...]).astype(BF16)
        k_ref[0, h, :, LANES:] = (kr_rot * k_inv + kb_ref[...]).astype(BF16)
        v_ref[0, h] = v_all[:, sl].T.astype(BF16)


def _mla_prep(z, gqa, gkva, wqn, wqr, wkn, wv, gqn, gqr, gkn, gkr, qb, kb, tabs, B, S, tm):
    T = z.shape[0]
    tpb = S // tm
    H = MLA_HEADS
    full = lambda a: pl.BlockSpec(a.shape, lambda i: (0,) * a.ndim)
    row = pl.BlockSpec((tm, LANES), lambda i: (i, 0))
    return pl.pallas_call(
        _mla_prep_kernel,
        grid=(T // tm,),
        in_specs=[pl.BlockSpec((tm, MLA_Q_RANK), lambda i: (i, C_QA // MLA_Q_RANK)),
                  pl.BlockSpec((tm, MLA_KV_RANK), lambda i: (i, C_KVA // MLA_KV_RANK)),
                  pl.BlockSpec((tm, LANES), lambda i: (i, C_KR // LANES)),
                  full(gqa), full(gkva), full(wqn), full(wqr), full(wkn), full(wv),
                  full(gqn), full(gqr), full(gkn), full(gkr), full(qb), full(kb), row, row, row],
        out_specs=[pl.BlockSpec((1, H, tm, QK_PAD), lambda i: (i // tpb, 0, i % tpb, 0)),
                   pl.BlockSpec((1, H, tm, QK_PAD), lambda i: (i // tpb, 0, i % tpb, 0)),
                   pl.BlockSpec((1, H, MLA_V, tm), lambda i: (i // tpb, 0, 0, i % tpb))],
        out_shape=[jax.ShapeDtypeStruct((B, H, S, QK_PAD), BF16),
                   jax.ShapeDtypeStruct((B, H, S, QK_PAD), BF16),
                   jax.ShapeDtypeStruct((B, H, MLA_V, S), BF16)],
        compiler_params=_cp("parallel"),
        name="mla_prep",
    )(z, z, z, gqa, gkva, wqn, wqr, wkn, wv, gqn, gqr, gkn, gkr, qb, kb, *tabs)


def _gelu_tanh(x):
    return 0.5 * x * (1.0 + jnp.tanh(0.7978845608028654 * (x + 0.044715 * (x * x * x))))


def _sg_kernel(zu_ref, zv_ref, g_ref, w_ref, b_ref, o_ref):
    tm = zu_ref.shape[0]
    r = lax.broadcasted_iota(jnp.int32, (SG_LEN, SG_LEN), 0)
    c = lax.broadcasted_iota(jnp.int32, (SG_LEN, SG_LEN), 1)
    causal = c <= r
    for g in range(SG_GROUPS):
        sl = slice(g * SG_CH, (g + 1) * SG_CH)
        wt = jnp.where(causal, w_ref[g], 0.0).astype(BF16)
        v = _gelu_tanh(zv_ref[:, sl].astype(F32))
        v = (v * _rms_scale(v, SG_CH) * g_ref[:, sl]).astype(BF16)
        u = _gelu_tanh(zu_ref[:, sl].astype(F32))
        for n in range(tm // SG_LEN):
            rows = slice(n * SG_LEN, (n + 1) * SG_LEN)
            s = _dot(wt, v[rows, :]) + b_ref[:, sl]
            o_ref[rows, sl] = (u[rows, :] * s).astype(BF16)


def _sg(z, gain, w, bias, tm):
    T = z.shape[0]
    return pl.pallas_call(
        _sg_kernel,
        grid=(T // tm,),
        in_specs=[pl.BlockSpec((tm, SG_WIDTH), lambda i: (i, C_SGU // SG_WIDTH)),
                  pl.BlockSpec((tm, SG_WIDTH), lambda i: (i, C_SGV // SG_WIDTH)),
                  pl.BlockSpec((1, SG_WIDTH), lambda i: (0, 0)),
                  pl.BlockSpec((SG_GROUPS, SG_LEN, SG_LEN), lambda i: (0, 0, 0)),
                  pl.BlockSpec((SG_LEN, SG_WIDTH), lambda i: (0, 0))],
        out_specs=pl.BlockSpec((tm, SG_WIDTH), lambda i: (i, 0)),
        out_shape=jax.ShapeDtypeStruct((T, SG_WIDTH), BF16),
        compiler_params=_cp("parallel"),
        name="spatial_gating",
    )(z, z, gain, w, bias)


def _flash_t(q, k_ref, vt_ref, n_off, diag_fixes, online):
    n = q.shape[0]
    qt = q.astype(F32).T.astype(BF16)

    def step(j, state, fix=None):
        m, l, acc = state
        start = pl.multiple_of(j * KEY_TILE, KEY_TILE)
        s = _dot(k_ref[0, 0, pl.ds(start, KEY_TILE), :], qt)
        if fix is not None:
            s = fix(s)
        if online:
            m_new = jnp.maximum(m, jnp.max(s, axis=0, keepdims=True))
            alpha = jnp.exp2(m - m_new)
            p = jnp.exp2(s - m_new)
            l = alpha * l
            acc = alpha * acc
        else:
            m_new = m
            p = jnp.exp2(s)
        l = l + jnp.sum(p, axis=0, keepdims=True)
        acc = acc + _dot(vt_ref[0, 0, :, pl.ds(start, KEY_TILE)], p.astype(BF16))
        return m_new, l, acc

    def group(g, state):
        for u in range(KV_UNROLL):
            state = step(g * KV_UNROLL + u, state)
        return state

    state = (jnp.full((1, n), NEG_BIG, F32), jnp.zeros((1, n), F32), jnp.zeros((DA_V, n), F32))
    n_grp = n_off // KV_UNROLL
    state = lax.fori_loop(0, n_grp, group, state)
    state = lax.fori_loop(n_grp * KV_UNROLL, n_off, step, state)
    for t, fix in enumerate(diag_fixes):
        state = step(n_off + t, state, fix)
    _, l, acc = state
    return acc / l


def _chunk_visible(n, tq, key_off):
    r = lax.broadcasted_iota(jnp.int32, (KEY_TILE, n), 0) + key_off
    c = lax.broadcasted_iota(jnp.int32, (KEY_TILE, n), 1)
    if n > tq:
        c = jnp.where(c >= tq, c - tq, c)
    shift = CHUNK.bit_length() - 1
    return lax.shift_right_logical(r, shift) <= lax.shift_right_logical(c, shift)


def _da_attn_kernel(q_ref, k_ref, vt_ref, dk_ref, dq_ref, sl2_ref, lam_ref, hg_ref, o_ref, *,
                    tq, lam_init, online):
    qi = pl.program_id(2)
    q = q_ref[0, 0].reshape(2 * tq, QK_PAD)
    dq = dq_ref[0]
    dq2 = jnp.concatenate([dq, dq], axis=1)
    corr = sl2_ref[0, :, :1] * jnp.maximum(dk_ref[...] - dq2, 0.0)
    visible = _chunk_visible(2 * tq, tq, 0)
    ot = _flash_t(q, k_ref, vt_ref, qi, [lambda s: jnp.where(visible, s - corr, NEG_BIG)], online)
    lv = lam_ref[...]
    lam = (jnp.exp(jnp.sum(lv[0:1] * lv[1:2], axis=-1, keepdims=True))
           - jnp.exp(jnp.sum(lv[2:3] * lv[3:4], axis=-1, keepdims=True)) + lam_init)
    od = ot[:, :tq] - lam * ot[:, tq:]
    inv = lax.rsqrt(jnp.sum(od * od, axis=0, keepdims=True) * (1.0 / DA_V) + EPS)
    od = od * inv * (1.0 - lam_init)
    o_ref[...] = (od.T * hg_ref[0]).astype(o_ref.dtype)


def _da_attention(q, k, vt, d_col, d_row, sl2, lam_vecs, hg, B, S, tq, lam_init, online):
    H = DA_HEADS
    nq = S // tq
    assert tq == KEY_TILE
    return pl.pallas_call(
        functools.partial(_da_attn_kernel, tq=tq, lam_init=lam_init, online=online),
        grid=(B, H, nq),
        in_specs=[pl.BlockSpec((1, 1, 2, tq, QK_PAD), lambda b, h, i: (b, h, 0, i, 0)),
                  pl.BlockSpec((1, 1, S, QK_PAD), lambda b, h, i: (b, h, 0, 0)),
                  pl.BlockSpec((1, 1, DA_V, S), lambda b, h, i: (b, h, 0, 0)),
                  pl.BlockSpec((tq, 1), lambda b, h, i: (b * nq + i, 0)),
                  pl.BlockSpec((1, 1, tq), lambda b, h, i: (b, 0, i)),
                  pl.BlockSpec((1, 1, LANES), lambda b, h, i: (h, 0, 0)),
                  pl.BlockSpec((4, DA_QK), lambda b, h, i: (0, 0)),
                  pl.BlockSpec((1, 1, DA_V), lambda b, h, i: (h, 0, 0))],
        out_specs=pl.BlockSpec((tq, DA_V), lambda b, h, i: (b * nq + i, h)),
        out_shape=jax.ShapeDtypeStruct((B * S, DA_WIDTH), BF16),
        compiler_params=_cp("parallel", "parallel", "parallel"),
        name="da_attention",
    )(q, k, vt, d_col, d_row, sl2, lam_vecs, hg)


def _mla_attn_kernel(q_ref, k_ref, vt_ref, o_ref, *, tq, online):
    qi = pl.program_id(2)
    tiles = tq // KEY_TILE
    fixes = []
    for t in range(tiles):
        visible = _chunk_visible(tq, tq, t * KEY_TILE)
        fixes.append(lambda s, vis=visible: jnp.where(vis, s, NEG_BIG))
    ot = _flash_t(q_ref[0, 0], k_ref, vt_ref, qi * tiles, fixes, online)
    o_ref[...] = ot.T.astype(o_ref.dtype)


def _mla_attention(q, k, vt, B, S, tq, online):
    H = MLA_HEADS
    nq = S // tq
    return pl.pallas_call(
        functools.partial(_mla_attn_kernel, tq=tq, online=online),
        grid=(B, H, nq),
        in_specs=[pl.BlockSpec((1, 1, tq, QK_PAD), lambda b, h, i: (b, h, i, 0)),
                  pl.BlockSpec((1, 1, S, QK_PAD), lambda b, h, i: (b, h, 0, 0)),
                  pl.BlockSpec((1, 1, MLA_V, S), lambda b, h, i: (b, h, 0, 0))],
        out_specs=pl.BlockSpec((tq, MLA_V), lambda b, h, i: (b * nq + i, h)),
        out_shape=jax.ShapeDtypeStruct((B * S, MLA_WIDTH), BF16),
        compiler_params=_cp("parallel", "parallel", "parallel"),
        name="mla_attention",
    )(q, k, vt)


def _out_proj_kernel(a_ref, b_ref, c_ref, wa_ref, wb_ref, wc_ref, x_ref, g_ref, o_ref):
    mix = _dot(a_ref[...], wa_ref[...]) + _dot(b_ref[...], wb_ref[...]) + _dot(c_ref[...], wc_ref[...])
    o_ref[...] = x_ref[...] + g_ref[0] * mix


def _out_proj(a, b, c, w, x2, mod_l, S, tm, tn):
    T, D = x2.shape
    tpb = S // tm
    nb = D // tn
    return pl.pallas_call(
        _out_proj_kernel,
        grid=(T // tm, nb),
        in_specs=[pl.BlockSpec((tm, DA_WIDTH), lambda i, j: (i, 0)),
                  pl.BlockSpec((tm, MLA_WIDTH), lambda i, j: (i, 0)),
                  pl.BlockSpec((tm, SG_WIDTH), lambda i, j: (i, 0)),
                  pl.BlockSpec((DA_WIDTH, tn), lambda i, j: (0, j)),
                  pl.BlockSpec((MLA_WIDTH, tn), lambda i, j: (1, j)),
                  pl.BlockSpec((SG_WIDTH, tn), lambda i, j: ((DA_WIDTH + MLA_WIDTH) // SG_WIDTH, j)),
                  pl.BlockSpec((tm, tn), lambda i, j: (i, j)),
                  pl.BlockSpec((1, 1, tn), lambda i, j: (i // tpb, 0, 2 * nb + j))],
        out_specs=pl.BlockSpec((tm, tn), lambda i, j: (i, j)),
        out_shape=jax.ShapeDtypeStruct((T, D), F32),
        compiler_params=_cp("parallel", "parallel"),
        name="out_proj",
    )(a, b, c, w, w, w, x2, mod_l)


def _ffn_kernel(x_ref, xh_ref, sh_ref, sc_ref, g_ref, wg_ref, wv_ref, cwg_ref, cwv_ref, cbg_ref, cbv_ref,
                wd_ref, o_ref, h_scr, *, tpb):
    i = pl.program_id(0)
    j = pl.program_id(1)

    @pl.when(j == 0)
    def _():
        def modnorm(x):
            return x * _rms_scale(x, x.shape[-1]) * (1.0 + sc_ref[0]) + sh_ref[0]

        for r in range(0, x_ref.shape[0], NORM_ROWS):
            h_scr[HALO + r:HALO + r + NORM_ROWS, :] = modnorm(x_ref[r:r + NORM_ROWS, :]).astype(BF16)
        keep = jnp.where(i % tpb == 0, 0.0, 1.0)
        h_scr[:HALO, :] = (modnorm(xh_ref[...]) * keep).astype(BF16)
        o_ref[...] = jnp.zeros_like(o_ref)

    h = h_scr[...]

    def conv(a, cw_ref, cb_ref):
        y = (cb_ref[...] + a * cw_ref[2:3, :] + pltpu.roll(a, 1, 0) * cw_ref[1:2, :]
             + pltpu.roll(a, 2, 0) * cw_ref[0:1, :])
        return y[HALO:, :]

    yg = conv(_dot(h, wg_ref[...]), cwg_ref, cbg_ref)
    yv = conv(_dot(h, wv_ref[...]), cwv_ref, cbv_ref)
    act = (yg / (1.0 + jnp.exp(-yg)) * yv).astype(BF16)
    o_ref[...] += _dot(act, wd_ref[...])

    @pl.when(j == pl.num_programs(1) - 1)
    def _():
        o_ref[...] = x_ref[...] + g_ref[0] * o_ref[...]


def _ffn(x2, mod_l, w_up, conv_w, conv_b, w_down, S, tm, tf):
    T, D = x2.shape
    F = w_down.shape[0]
    nf = F // tf
    tpb = S // tm
    hb = tm // HALO
    return pl.pallas_call(
        functools.partial(_ffn_kernel, tpb=tpb),
        grid=(T // tm, nf),
        in_specs=[pl.BlockSpec((tm, D), lambda i, j: (i, 0)),
                  pl.BlockSpec((HALO, D), lambda i, j: (jnp.maximum(i * hb - 1, 0), 0)),
                  pl.BlockSpec((1, 1, D), lambda i, j: (i // tpb, 0, 3)),
                  pl.BlockSpec((1, 1, D), lambda i, j: (i // tpb, 0, 4)),
                  pl.BlockSpec((1, 1, D), lambda i, j: (i // tpb, 0, 5)),
                  pl.BlockSpec((D, tf), lambda i, j: (0, j)),
                  pl.BlockSpec((D, tf), lambda i, j: (0, nf + j)),
                  pl.BlockSpec((CONV_W, tf), lambda i, j: (0, j)),
                  pl.BlockSpec((CONV_W, tf), lambda i, j: (0, nf + j)),
                  pl.BlockSpec((1, tf), lambda i, j: (0, j)),
                  pl.BlockSpec((1, tf), lambda i, j: (0, nf + j)),
                  pl.BlockSpec((tf, D), lambda i, j: (j, 0))],
        out_specs=pl.BlockSpec((tm, D), lambda i, j: (i, 0), pipeline_mode=pl.Buffered(1)),
        out_shape=jax.ShapeDtypeStruct((T, D), F32),
        scratch_shapes=[pltpu.VMEM((HALO + tm, D), BF16)],
        compiler_params=_cp("parallel", "arbitrary"),
        name="conv_ffn",
    )(x2, x2, mod_l, mod_l, mod_l, w_up, w_up, conv_w, conv_w, conv_b, conv_b, w_down)


def _split3(v):
    a = v.astype(BF16).astype(F32)
    b = (v - a).astype(BF16).astype(F32)
    c = (v - a - b).astype(BF16).astype(F32)
    return a, b, c


def kernel(x, c, positions, w_ada, b_ada, w_in, da_q_gain, da_k_gain, da_lq1, da_lk1, da_lq2, da_lk2, da_head_gain, mla_q_a_gain, mla_w_uq, mla_kv_a_gain, mla_w_ukv, mla_q_gain, mla_k_gain, sg_v_gain, sg_w, sg_b, w_out, ffn_w_up, ffn_conv_w, ffn_conv_b, ffn_w_down):
    B, S, D = x.shape
    L = w_ada.shape[0]
    T = B * S
    assert S % 512 == 0 and D % 1024 == 0
    tm_big = min(S, 1024)
    tm = 512
    tq_mla = min(S, 1024)

    mod = _ada_mod(c, w_ada, b_ada)

    d = (positions - positions[:, :1]).astype(F32)
    d_col = d.reshape(T, 1)
    d_row = d.reshape(B, 1, S)
    tabs = _rope_tables(positions.astype(F32).reshape(T, 1))

    H = DA_HEADS
    sl = jnp.asarray(ALIBI_SLOPES, F32) * LOG2E
    slope6 = jnp.tile(jnp.stack(_split3(sl), axis=-1), (1, 2))
    zeros = lambda n: jnp.zeros((H, n), F32)
    kc = jnp.concatenate([zeros(6), slope6, jnp.ones((H, 3), F32), zeros(LANES - 15)], axis=-1)
    kc = kc.reshape(H, 1, LANES)
    sl2 = jnp.broadcast_to((2.0 * sl)[:, None, None], (H, 1, LANES))

    da_scale = DA_QK ** -0.5 * LOG2E
    mla_width = MLA_NOPE + MLA_ROPE
    mla_scale = mla_width ** -0.5 * LOG2E
    pad_r = lambda g: jnp.pad(g, (0, LANES - MLA_ROPE)).reshape(1, LANES)
    shift_lanes = lambda v: jnp.pad(v, (MLA_ROPE, LANES - MLA_ROPE - 3)).reshape(1, LANES)
    kb_mla = shift_lanes(jnp.ones((3,), F32))
    gmax = lambda g: jnp.max(jnp.abs(g))

    x2 = x.reshape(T, D)
    for l in range(L):
        mod_l = mod[l].reshape(B, 1, 6 * D)
        wi = w_in[l]
        w_in_p = jnp.concatenate(
            [wi[:, 2304:2816], wi[:, 3136:3648], wi[:, 3648:4160], wi[:, 0:768], wi[:, 768:1536],
             wi[:, 1536:2304], wi[:, 2816:3072], wi[:, 3072:3136],
             jnp.zeros((D, IN_PAD - C_KR - MLA_ROPE), wi.dtype)], axis=1).astype(BF16)
        z = _norm_matmul(x2, mod_l, 0, w_in_p, S, tm_big, IN_TILE)

        lam_init = 0.8 - 0.6 * math.exp(-0.3 * l)
        gq = (jnp.tile(da_q_gain[l], 2) * da_scale).reshape(1, LANES)
        gk = jnp.tile(da_k_gain[l], 2).reshape(1, LANES)
        bound = BOUND_SLACK * DA_QK * da_scale * gmax(da_q_gain[l]) * gmax(da_k_gain[l])
        nb = jnp.broadcast_to(-jnp.stack(_split3(bound)), (H, 3))
        qc = jnp.concatenate([slope6, zeros(6), nb, zeros(LANES - 15)], axis=-1).reshape(H, 1, LANES)
        q_da, k_da, vt_da = _da_prep(z, d_col, gq, gk, qc, kc, B, S, tm)
        lam_vecs = jnp.stack([da_lq1[l], da_lk1[l], da_lq2[l], da_lk2[l]])
        da_args = (q_da, k_da, vt_da, d_col, d_row, sl2, lam_vecs, da_head_gain[l].reshape(H, 1, DA_V))
        out_a = lax.cond(
            2.0 * bound <= MAX_SHIFT,
            lambda a: _da_attention(*a, B, S, KEY_TILE, lam_init, online=False),
            lambda a: _da_attention(*a, B, S, KEY_TILE, lam_init, online=True), da_args)

        wq = mla_w_uq[l].reshape(MLA_Q_RANK, MLA_HEADS, MLA_NOPE + MLA_ROPE)
        wqn = wq[:, :, :MLA_NOPE].reshape(MLA_Q_RANK, MLA_WIDTH).astype(BF16)
        wqr = jnp.pad(wq[:, :, MLA_NOPE:], ((0, 0), (0, 0), (0, LANES - MLA_ROPE))
                      ).reshape(MLA_Q_RANK, MLA_HEADS * LANES).astype(BF16)
        wkv = mla_w_ukv[l].reshape(MLA_KV_RANK, MLA_HEADS, MLA_NOPE + MLA_V)
        wkn = wkv[:, :, :MLA_NOPE].reshape(MLA_KV_RANK, MLA_WIDTH).astype(BF16)
        wv = wkv[:, :, MLA_NOPE:].reshape(MLA_KV_RANK, MLA_WIDTH).astype(BF16)
        qg, kg = mla_q_gain[l], mla_k_gain[l]
        bound_m = BOUND_SLACK * mla_width * mla_scale * gmax(qg) * gmax(kg)
        q_m, k_m, v_m = _mla_prep(
            z, mla_q_a_gain[l].reshape(1, -1), mla_kv_a_gain[l].reshape(1, -1), wqn, wqr, wkn, wv,
            (qg[:MLA_NOPE] * mla_scale).reshape(1, LANES), pad_r(qg[MLA_NOPE:] * mla_scale),
            kg[:MLA_NOPE].reshape(1, LANES), pad_r(kg[MLA_NOPE:]),
            shift_lanes(-jnp.stack(_split3(bound_m))), kb_mla, tabs, B, S, tm)
        out_b = lax.cond(
            2.0 * bound_m <= MAX_SHIFT,
            lambda a: _mla_attention(*a, B, S, tq_mla, online=False),
            lambda a: _mla_attention(*a, B, S, tq_mla, online=True), (q_m, k_m, v_m))

        sg_bias = jnp.repeat(sg_b[l].T, SG_CH, axis=1)
        out_c = _sg(z, sg_v_gain[l].reshape(1, SG_WIDTH), sg_w[l], sg_bias, tm)

        x2 = _out_proj(out_a, out_b, out_c, w_out[l].astype(BF16), x2, mod_l, S, tm, D)
        x2 = _ffn(x2, mod_l, ffn_w_up[l].astype(BF16), ffn_conv_w[l], ffn_conv_b[l].reshape(1, -1),
                  ffn_w_down[l].astype(BF16), S, tm_big, 512)
    return x2.reshape(B, S, D)
```

```python
import functools
import math

import jax
import jax.numpy as jnp
from jax import lax
from jax.experimental import pallas as pl
from jax.experimental.pallas import tpu as pltpu

CHUNK = 64
EPS = 1e-6
DA_HEADS = 6
DA_QK = 64
DA_V = 128
MLA_HEADS = 6
MLA_Q_RANK = 512
MLA_KV_RANK = 256
MLA_NOPE = 128
MLA_ROPE = 64
MLA_V = 128
ROPE_THETA = 10000.0
SG_GROUPS = 4
SG_CH = 128
SG_LEN = 128
CONV_W = 3
DA_WIDTH = DA_HEADS * DA_V
MLA_WIDTH = MLA_HEADS * MLA_V
SG_WIDTH = SG_GROUPS * SG_CH
ALIBI_SLOPES = tuple(2.0 ** (-8.0 * (h + 1) / DA_HEADS) for h in range(DA_HEADS))

LOG2E = 1.4426950408889634
LANES = 128
QK_PAD = 256
NEG_BIG = -1e30
MAX_SHIFT = 100.0
BOUND_SLACK = 1.02
KV_UNROLL = 4
KEY_TILE = 512
FFN_TILE = 512
FFN_ROWS = 1024
ROW_TILE = 512
MLA_QUERIES = 1024
HALO = 16
NORM_ROWS = 256
VMEM_LIMIT = 56 * 1024 * 1024

C_QA, C_SGU, C_SGV, C_DAQ, C_DAK, C_DAV, C_KVA, C_KR, IN_PAD = (
    0, 512, 1024, 1536, 2304, 3072, 3840, 4096, 4224)

BF16 = jnp.bfloat16
F32 = jnp.float32


def _cp(*sem):
    return pltpu.CompilerParams(dimension_semantics=sem, vmem_limit_bytes=VMEM_LIMIT)


def _dot(a, b):
    return jnp.dot(a, b, preferred_element_type=F32)


def _dot_nt(a, b):
    return lax.dot_general(a, b, (((1,), (1,)), ((), ())), preferred_element_type=F32)


def _rms_scale(x, n):
    return lax.rsqrt(jnp.sum(x * x, axis=-1, keepdims=True) * (1.0 / n) + EPS)


def _same_group_matrix(group):
    r = lax.broadcasted_iota(jnp.int32, (LANES, LANES), 0)
    c = lax.broadcasted_iota(jnp.int32, (LANES, LANES), 1)
    shift = group.bit_length() - 1
    same = lax.shift_right_logical(r, shift) == lax.shift_right_logical(c, shift)
    return jnp.where(same, 1.0, 0.0).astype(BF16)


def _ada_kernel(ct_ref, w_ref, b_ref, o_ref):
    c = ct_ref[...]
    cond = c / (1.0 + jnp.exp(-c))
    w = w_ref[0]
    for b in range(c.shape[1]):
        o_ref[0, b:b + 1, :] = jnp.sum(w * cond[:, b:b + 1], axis=0, keepdims=True) + b_ref[0]


def _ada_mod(c, w_ada, b_ada):
    L, D, N = w_ada.shape
    B = c.shape[0]
    tn = 1024
    return pl.pallas_call(
        _ada_kernel,
        grid=(L, N // tn),
        in_specs=[pl.BlockSpec((D, B), lambda l, j: (0, 0)),
                  pl.BlockSpec((1, D, tn), lambda l, j: (l, 0, j)),
                  pl.BlockSpec((1, 1, tn), lambda l, j: (l, 0, j))],
        out_specs=pl.BlockSpec((1, B, tn), lambda l, j: (l, 0, j)),
        out_shape=jax.ShapeDtypeStruct((L, B, N), F32),
        compiler_params=_cp("parallel", "parallel"),
        name="ada_mod",
    )(c.T, w_ada, b_ada.reshape(L, 1, N))


def _rope_table_kernel(pos_ref, inv_ref, c_ref, s1_ref, s2_ref):
    ang = pos_ref[...] * inv_ref[...]
    lane = lax.broadcasted_iota(jnp.int32, ang.shape, 1)
    half = MLA_ROPE // 2
    cos = jnp.cos(ang)
    sin = jnp.sin(ang)
    c_ref[...] = jnp.where(lane < MLA_ROPE, cos, 0.0)
    s1_ref[...] = jnp.where(lane < half, -sin, 0.0)
    s2_ref[...] = jnp.where((lane >= half) & (lane < MLA_ROPE), sin, 0.0)


def _rope_tables(pos_col):
    T = pos_col.shape[0]
    half = MLA_ROPE // 2
    inv = ROPE_THETA ** (-jnp.arange(half, dtype=F32) / half)
    inv_row = jnp.concatenate([inv, inv, jnp.zeros((LANES - MLA_ROPE,), F32)]).reshape(1, LANES)
    tm = min(T, 2048)
    sds = jax.ShapeDtypeStruct((T, LANES), F32)
    spec = pl.BlockSpec((tm, LANES), lambda i: (i, 0))
    return pl.pallas_call(
        _rope_table_kernel,
        grid=(T // tm,),
        in_specs=[pl.BlockSpec((tm, 1), lambda i: (i, 0)),
                  pl.BlockSpec((1, LANES), lambda i: (0, 0))],
        out_specs=[spec, spec, spec],
        out_shape=[sds, sds, sds],
        compiler_params=_cp("parallel"),
        name="rope_tables",
    )(pos_col, inv_row)


def _norm_matmul_kernel(x_ref, sh_ref, sc_ref, w_ref, o_ref, h_scr):
    @pl.when(pl.program_id(1) == 0)
    def _():
        for r in range(0, x_ref.shape[0], NORM_ROWS):
            x = x_ref[r:r + NORM_ROWS, :]
            h = x * _rms_scale(x, x.shape[-1]) * (1.0 + sc_ref[0]) + sh_ref[0]
            h_scr[r:r + NORM_ROWS, :] = h.astype(BF16)

    o_ref[...] = _dot(h_scr[...], w_ref[...]).astype(o_ref.dtype)


def _norm_matmul(x2, mod_l, which, w, S, tm, tn):
    T, D = x2.shape
    N = w.shape[1]
    tpb = S // tm
    w_mode = dict(pipeline_mode=pl.Buffered(1)) if tn == N else {}
    return pl.pallas_call(
        _norm_matmul_kernel,
        grid=(T // tm, N // tn),
        in_specs=[pl.BlockSpec((tm, D), lambda i, j: (i, 0)),
                  pl.BlockSpec((1, 1, D), lambda i, j: (i // tpb, 0, which)),
                  pl.BlockSpec((1, 1, D), lambda i, j: (i // tpb, 0, which + 1)),
                  pl.BlockSpec((D, tn), lambda i, j: (0, j), **w_mode)],
        out_specs=pl.BlockSpec((tm, tn), lambda i, j: (i, j)),
        out_shape=jax.ShapeDtypeStruct((T, N), BF16),
        scratch_shapes=[pltpu.VMEM((tm, D), BF16)],
        compiler_params=_cp("parallel", "arbitrary"),
        name="norm_matmul",
    )(x2, mod_l, mod_l, w)


def _da_prep_kernel(zq_ref, zk_ref, zv_ref, d_ref, gq_ref, gk_ref, qc_ref, kc_ref, q_ref, k_ref, vt_ref):
    lane = lax.broadcasted_iota(jnp.int32, (1, LANES), 1)
    lo = lane < DA_QK
    same_map = _same_group_matrix(DA_QK)

    def qk_norm(z, g):
        ss = _dot((z * z).astype(BF16), same_map)
        return z * lax.rsqrt(ss * (1.0 / DA_QK) + EPS) * g

    d = d_ref[...]
    d_hi = jnp.floor(d * (1.0 / LANES)) * LANES
    d_lo = d - d_hi
    k_pos = jnp.where(lane < 3, d_hi, jnp.where(lane < 6, d_lo, 0.0))
    q_pos = jnp.where((lane >= 6) & (lane < 9), -d_hi, jnp.where((lane >= 9) & (lane < 12), -d_lo, 0.0))
    for h in range(DA_HEADS):
        sl = slice(h * LANES, (h + 1) * LANES)
        qn = qk_norm(zq_ref[:, sl].astype(F32), gq_ref[...])
        kn = qk_norm(zk_ref[:, sl].astype(F32), gk_ref[...])
        qa = (q_pos + qc_ref[h]).astype(BF16)
        q_ref[0, h, 0, :, :LANES] = jnp.where(lo, qn, 0.0).astype(BF16)
        q_ref[0, h, 0, :, LANES:] = qa
        q_ref[0, h, 1, :, :LANES] = jnp.where(lo, 0.0, qn).astype(BF16)
        q_ref[0, h, 1, :, LANES:] = qa
        k_ref[0, h, :, :LANES] = kn.astype(BF16)
        k_ref[0, h, :, LANES:] = (k_pos + kc_ref[h]).astype(BF16)
        vt_ref[0, h] = zv_ref[:, sl].astype(F32).T.astype(BF16)


def _da_prep(z, d_col, gq, gk, qc, kc, B, S, tm):
    T = z.shape[0]
    tpb = S // tm
    H = DA_HEADS
    return pl.pallas_call(
        _da_prep_kernel,
        grid=(T // tm,),
        in_specs=[pl.BlockSpec((tm, DA_WIDTH), lambda i: (i, C_DAQ // DA_WIDTH)),
                  pl.BlockSpec((tm, DA_WIDTH), lambda i: (i, C_DAK // DA_WIDTH)),
                  pl.BlockSpec((tm, DA_WIDTH), lambda i: (i, C_DAV // DA_WIDTH)),
                  pl.BlockSpec((tm, 1), lambda i: (i, 0)),
                  pl.BlockSpec((1, LANES), lambda i: (0, 0)),
                  pl.BlockSpec((1, LANES), lambda i: (0, 0)),
                  pl.BlockSpec((H, 1, LANES), lambda i: (0, 0, 0)),
                  pl.BlockSpec((H, 1, LANES), lambda i: (0, 0, 0))],
        out_specs=[pl.BlockSpec((1, H, 2, tm, QK_PAD), lambda i: (i // tpb, 0, 0, i % tpb, 0)),
                   pl.BlockSpec((1, H, tm, QK_PAD), lambda i: (i // tpb, 0, i % tpb, 0)),
                   pl.BlockSpec((1, H, DA_V, tm), lambda i: (i // tpb, 0, 0, i % tpb))],
        out_shape=[jax.ShapeDtypeStruct((B, H, 2, S, QK_PAD), BF16),
                   jax.ShapeDtypeStruct((B, H, S, QK_PAD), BF16),
                   jax.ShapeDtypeStruct((B, H, DA_V, S), BF16)],
        compiler_params=_cp("parallel"),
        name="da_prep",
    )(z, z, z, d_col, gq, gk, qc, kc)


def _mla_prep_kernel(zqa_ref, zkva_ref, zkr_ref, gqa_ref, gkva_ref, wqn_ref, wqr_ref, wkn_ref, wv_ref,
                     gqn_ref, gqr_ref, gkn_ref, gkr_ref, qb_ref, kb_ref, c_ref, s1_ref, s2_ref,
                     q_ref, k_ref, v_ref):
    half = MLA_ROPE // 2
    width = MLA_NOPE + MLA_ROPE
    cos, sin1, sin2 = c_ref[...], s1_ref[...], s2_ref[...]

    def rope(t):
        return (t * cos + pltpu.roll(t, LANES - half, 1) * sin1 + pltpu.roll(t, half, 1) * sin2)

    zqa = zqa_ref[...].astype(F32)
    qa = (zqa * _rms_scale(zqa, MLA_Q_RANK) * gqa_ref[...]).astype(BF16)
    zkva = zkva_ref[...].astype(F32)
    kva = (zkva * _rms_scale(zkva, MLA_KV_RANK) * gkva_ref[...]).astype(BF16)
    qn_all = _dot(qa, wqn_ref[...])
    qr_all = _dot(qa, wqr_ref[...])
    kn_all = _dot(kva, wkn_ref[...])
    v_all = _dot(kva, wv_ref[...])
    kr = zkr_ref[...].astype(F32)
    kr_sq = kr * kr
    kr_rot = rope(kr * gkr_ref[...])
    ones = _same_group_matrix(LANES)
    for h in range(MLA_HEADS):
        sl = slice(h * LANES, (h + 1) * LANES)
        qn, qr, kn = qn_all[:, sl], qr_all[:, sl], kn_all[:, sl]
        q_inv = lax.rsqrt(_dot((qn * qn + qr * qr).astype(BF16), ones) * (1.0 / width) + EPS)
        k_inv = lax.rsqrt(_dot((kn * kn + kr_sq).astype(BF16), ones) * (1.0 / width) + EPS)
        q_ref[0, h, :, :LANES] = (qn * q_inv * gqn_ref[...]).astype(BF16)
        q_ref[0, h, :, LANES:] = (rope(qr * q_inv * gqr_ref[...]) + qb_ref[...]).astype(BF16)
        k_ref[0, h, :, :LANES] = (kn * k_inv * gkn_ref[...]).astype(BF16)
        k_ref[0, h, :, LANES:] = (kr_rot * k_inv + kb_ref[...]).astype(BF16)
        v_ref[0, h] = v_all[:, sl].T.astype(BF16)


def _mla_prep(z, gqa, gkva, wqn, wqr, wkn, wv, gqn, gqr, gkn, gkr, qb, kb, tabs, B, S, tm):
    T = z.shape[0]
    tpb = S // tm
    H = MLA_HEADS
    full = lambda a: pl.BlockSpec(a.shape, lambda i: (0,) * a.ndim)
    row = pl.BlockSpec((tm, LANES), lambda i: (i, 0))
    return pl.pallas_call(
        _mla_prep_kernel,
        grid=(T // tm,),
        in_specs=[pl.BlockSpec((tm, MLA_Q_RANK), lambda i: (i, C_QA // MLA_Q_RANK)),
                  pl.BlockSpec((tm, MLA_KV_RANK), lambda i: (i, C_KVA // MLA_KV_RANK)),
                  pl.BlockSpec((tm, LANES), lambda i: (i, C_KR // LANES)),
                  full(gqa), full(gkva), full(wqn), full(wqr), full(wkn), full(wv),
                  full(gqn), full(gqr), full(gkn), full(gkr), full(qb), full(kb), row, row, row],
        out_specs=[pl.BlockSpec((1, H, tm, QK_PAD), lambda i: (i // tpb, 0, i % tpb, 0)),
                   pl.BlockSpec((1, H, tm, QK_PAD), lambda i: (i // tpb, 0, i % tpb, 0)),
                   pl.BlockSpec((1, H, MLA_V, tm), lambda i: (i // tpb, 0, 0, i % tpb))],
        out_shape=[jax.ShapeDtypeStruct((B, H, S, QK_PAD), BF16),
                   jax.ShapeDtypeStruct((B, H, S, QK_PAD), BF16),
                   jax.ShapeDtypeStruct((B, H, MLA_V, S), BF16)],
        compiler_params=_cp("parallel"),
        name="mla_prep",
    )(z, z, z, gqa, gkva, wqn, wqr, wkn, wv, gqn, gqr, gkn, gkr, qb, kb, *tabs)


def _gelu_tanh(x):
    return 0.5 * x * (1.0 + jnp.tanh(0.7978845608028654 * (x + 0.044715 * (x * x * x))))


def _sg_kernel(zu_ref, zv_ref, g_ref, w_ref, b_ref, o_ref):
    tm = zu_ref.shape[0]
    r = lax.broadcasted_iota(jnp.int32, (SG_LEN, SG_LEN), 0)
    c = lax.broadcasted_iota(jnp.int32, (SG_LEN, SG_LEN), 1)
    causal = c <= r
    for g in range(SG_GROUPS):
        sl = slice(g * SG_CH, (g + 1) * SG_CH)
        wt = jnp.where(causal, w_ref[g], 0.0).astype(BF16)
        v = _gelu_tanh(zv_ref[:, sl].astype(F32))
        v = (v * _rms_scale(v, SG_CH) * g_ref[:, sl]).astype(BF16)
        u = _gelu_tanh(zu_ref[:, sl].astype(F32))
        for n in range(tm // SG_LEN):
            rows = slice(n * SG_LEN, (n + 1) * SG_LEN)
            s = _dot(wt, v[rows, :]) + b_ref[:, sl]
            o_ref[rows, sl] = (u[rows, :] * s).astype(BF16)


def _sg(z, gain, w, bias, tm):
    T = z.shape[0]
    return pl.pallas_call(
        _sg_kernel,
        grid=(T // tm,),
        in_specs=[pl.BlockSpec((tm, SG_WIDTH), lambda i: (i, C_SGU // SG_WIDTH)),
                  pl.BlockSpec((tm, SG_WIDTH), lambda i: (i, C_SGV // SG_WIDTH)),
                  pl.BlockSpec((1, SG_WIDTH), lambda i: (0, 0)),
                  pl.BlockSpec((SG_GROUPS, SG_LEN, SG_LEN), lambda i: (0, 0, 0)),
                  pl.BlockSpec((SG_LEN, SG_WIDTH), lambda i: (0, 0))],
        out_specs=pl.BlockSpec((tm, SG_WIDTH), lambda i: (i, 0)),
        out_shape=jax.ShapeDtypeStruct((T, SG_WIDTH), BF16),
        compiler_params=_cp("parallel"),
        name="spatial_gating",
    )(z, z, gain, w, bias)


def _flash_t(q, k_ref, vt_ref, n_off, diag_fixes, online):
    n = q.shape[0]
    qt = q.astype(F32).T.astype(BF16)

    def step(j, state, fix=None):
        m, l, acc = state
        start = pl.multiple_of(j * KEY_TILE, KEY_TILE)
        s = _dot(k_ref[0, 0, pl.ds(start, KEY_TILE), :], qt)
        if fix is not None:
            s = fix(s)
        if online:
            m_new = jnp.maximum(m, jnp.max(s, axis=0, keepdims=True))
            alpha = jnp.exp2(m - m_new)
            p = jnp.exp2(s - m_new)
            l = alpha * l
            acc = alpha * acc
        else:
            m_new = m
            p = jnp.exp2(s)
        l = l + jnp.sum(p, axis=0, keepdims=True)
        acc = acc + _dot(vt_ref[0, 0, :, pl.ds(start, KEY_TILE)], p.astype(BF16))
        return m_new, l, acc

    def run(j0, count, state):
        for u in range(count):
            state = step(j0 + u, state)
        return state

    state = (jnp.full((1, n), NEG_BIG, F32), jnp.zeros((1, n), F32), jnp.zeros((DA_V, n), F32))
    n_grp = n_off // KV_UNROLL
    state = lax.fori_loop(0, n_grp, lambda g, st: run(g * KV_UNROLL, KV_UNROLL, st), state)
    state = lax.fori_loop(n_grp * KV_UNROLL, n_off, lambda j, st: run(j, 1, st), state)
    for t, fix in enumerate(diag_fixes):
        state = step(n_off + t, state, fix)
    _, l, acc = state
    return acc / l


def _chunk_visible(n, tq, key_off):
    r = lax.broadcasted_iota(jnp.int32, (KEY_TILE, n), 0) + key_off
    c = lax.broadcasted_iota(jnp.int32, (KEY_TILE, n), 1)
    if n > tq:
        c = jnp.where(c >= tq, c - tq, c)
    shift = CHUNK.bit_length() - 1
    return lax.shift_right_logical(r, shift) <= lax.shift_right_logical(c, shift)


def _da_attn_kernel(q_ref, k_ref, vt_ref, dk_ref, dq_ref, sl2_ref, lam_ref, hg_ref, o_ref, *,
                    tq, lam_init, online):
    qi = pl.program_id(2)
    q = q_ref[0, 0].reshape(2 * tq, QK_PAD)
    def fix(s):
        dq = dq_ref[0]
        dq2 = jnp.concatenate([dq, dq], axis=1)
        corr = sl2_ref[0, :, :1] * jnp.maximum(dk_ref[...] - dq2, 0.0)
        return jnp.where(_chunk_visible(2 * tq, tq, 0), s - corr, NEG_BIG)

    ot = _flash_t(q, k_ref, vt_ref, qi, [fix], online)
    lv = lam_ref[...]
    lam = (jnp.exp(jnp.sum(lv[0:1] * lv[1:2], axis=-1, keepdims=True))
           - jnp.exp(jnp.sum(lv[2:3] * lv[3:4], axis=-1, keepdims=True)) + lam_init)
    od = ot[:, :tq] - lam * ot[:, tq:]
    inv = lax.rsqrt(jnp.sum(od * od, axis=0, keepdims=True) * (1.0 / DA_V) + EPS)
    od = od * inv * (1.0 - lam_init)
    o_ref[...] = (od.T * hg_ref[0]).astype(o_ref.dtype)


def _da_attention(q, k, vt, d_col, d_row, sl2, lam_vecs, hg, B, S, tq, lam_init, online):
    H = DA_HEADS
    nq = S // tq
    assert tq == KEY_TILE
    return pl.pallas_call(
        functools.partial(_da_attn_kernel, tq=tq, lam_init=lam_init, online=online),
        grid=(B, H, nq),
        in_specs=[pl.BlockSpec((1, 1, 2, tq, QK_PAD), lambda b, h, i: (b, h, 0, i, 0)),
                  pl.BlockSpec((1, 1, S, QK_PAD), lambda b, h, i: (b, h, 0, 0)),
                  pl.BlockSpec((1, 1, DA_V, S), lambda b, h, i: (b, h, 0, 0)),
                  pl.BlockSpec((tq, 1), lambda b, h, i: (b * nq + i, 0)),
                  pl.BlockSpec((1, 1, tq), lambda b, h, i: (b, 0, i)),
                  pl.BlockSpec((1, 1, LANES), lambda b, h, i: (h, 0, 0)),
                  pl.BlockSpec((4, DA_QK), lambda b, h, i: (0, 0)),
                  pl.BlockSpec((1, 1, DA_V), lambda b, h, i: (h, 0, 0))],
        out_specs=pl.BlockSpec((tq, DA_V), lambda b, h, i: (b * nq + i, h)),
        out_shape=jax.ShapeDtypeStruct((B * S, DA_WIDTH), BF16),
        compiler_params=_cp("parallel", "parallel", "parallel"),
        name="da_attention",
    )(q, k, vt, d_col, d_row, sl2, lam_vecs, hg)


def _mla_attn_kernel(q_ref, k_ref, vt_ref, o_ref, *, tq, online):
    qi = pl.program_id(2)
    tiles = tq // KEY_TILE
    fixes = [lambda s, t=t: jnp.where(_chunk_visible(tq, tq, t * KEY_TILE), s, NEG_BIG) for t in range(tiles)]
    ot = _flash_t(q_ref[0, 0], k_ref, vt_ref, qi * tiles, fixes, online)
    o_ref[...] = ot.T.astype(o_ref.dtype)


def _mla_attention(q, k, vt, B, S, tq, online):
    H = MLA_HEADS
    nq = S // tq
    return pl.pallas_call(
        functools.partial(_mla_attn_kernel, tq=tq, online=online),
        grid=(B, H, nq),
        in_specs=[pl.BlockSpec((1, 1, tq, QK_PAD), lambda b, h, i: (b, h, i, 0)),
                  pl.BlockSpec((1, 1, S, QK_PAD), lambda b, h, i: (b, h, 0, 0)),
                  pl.BlockSpec((1, 1, MLA_V, S), lambda b, h, i: (b, h, 0, 0))],
        out_specs=pl.BlockSpec((tq, MLA_V), lambda b, h, i: (b * nq + i, h)),
        out_shape=jax.ShapeDtypeStruct((B * S, MLA_WIDTH), BF16),
        compiler_params=_cp("parallel", "parallel", "parallel"),
        name="mla_attention",
    )(q, k, vt)


def _out_proj_kernel(a_ref, b_ref, c_ref, wa_ref, wb_ref, wc_ref, x_ref, g_ref, o_ref):
    mix = _dot(a_ref[...], wa_ref[...]) + _dot(b_ref[...], wb_ref[...]) + _dot(c_ref[...], wc_ref[...])
    o_ref[...] = x_ref[...] + g_ref[0] * mix


def _out_proj(a, b, c, w, x2, mod_l, S, tm, tn):
    T, D = x2.shape
    tpb = S // tm
    nb = D // tn
    return pl.pallas_call(
        _out_proj_kernel,
        grid=(T // tm, nb),
        in_specs=[pl.BlockSpec((tm, DA_WIDTH), lambda i, j: (i, 0)),
                  pl.BlockSpec((tm, MLA_WIDTH), lambda i, j: (i, 0)),
                  pl.BlockSpec((tm, SG_WIDTH), lambda i, j: (i, 0)),
                  pl.BlockSpec((DA_WIDTH, tn), lambda i, j: (0, j)),
                  pl.BlockSpec((MLA_WIDTH, tn), lambda i, j: (1, j)),
                  pl.BlockSpec((SG_WIDTH, tn), lambda i, j: ((DA_WIDTH + MLA_WIDTH) // SG_WIDTH, j)),
                  pl.BlockSpec((tm, tn), lambda i, j: (i, j)),
                  pl.BlockSpec((1, 1, tn), lambda i, j: (i // tpb, 0, 2 * nb + j))],
        out_specs=pl.BlockSpec((tm, tn), lambda i, j: (i, j)),
        out_shape=jax.ShapeDtypeStruct((T, D), F32),
        compiler_params=_cp("parallel", "parallel"),
        name="out_proj",
    )(a, b, c, w, w, w, x2, mod_l)


def _ffn_kernel(x_ref, xh_ref, sh_ref, sc_ref, g_ref, wg_ref, wv_ref, cwg_ref, cwv_ref, cbg_ref, cbv_ref,
                wd_ref, o_ref, h_scr, *, tpb):
    i = pl.program_id(0)
    j = pl.program_id(1)

    @pl.when(j == 0)
    def _():
        def modnorm(x):
            return x * _rms_scale(x, x.shape[-1]) * (1.0 + sc_ref[0]) + sh_ref[0]

        for r in range(0, x_ref.shape[0], NORM_ROWS):
            h_scr[HALO + r:HALO + r + NORM_ROWS, :] = modnorm(x_ref[r:r + NORM_ROWS, :]).astype(BF16)
        keep = jnp.where(i % tpb == 0, 0.0, 1.0)
        h_scr[:HALO, :] = (modnorm(xh_ref[...]) * keep).astype(BF16)
        o_ref[...] = jnp.zeros_like(o_ref)

    h = h_scr[...]

    def conv(a, cw_ref, cb_ref):
        y = (cb_ref[...] + a * cw_ref[2:3, :] + pltpu.roll(a, 1, 0) * cw_ref[1:2, :]
             + pltpu.roll(a, 2, 0) * cw_ref[0:1, :])
        return y[HALO:, :]

    yg = conv(_dot(h, wg_ref[...]), cwg_ref, cbg_ref)
    yv = conv(_dot(h, wv_ref[...]), cwv_ref, cbv_ref)
    act = (yg / (1.0 + jnp.exp(-yg)) * yv).astype(BF16)
    o_ref[...] += _dot(act, wd_ref[...])

    @pl.when(j == pl.num_programs(1) - 1)
    def _():
        o_ref[...] = x_ref[...] + g_ref[0] * o_ref[...]


def _ffn(x2, mod_l, w_up, conv_w, conv_b, w_down, S, tm, tf):
    T, D = x2.shape
    F = w_down.shape[0]
    nf = F // tf
    tpb = S // tm
    hb = tm // HALO
    return pl.pallas_call(
        functools.partial(_ffn_kernel, tpb=tpb),
        grid=(T // tm, nf),
        in_specs=[pl.BlockSpec((tm, D), lambda i, j: (i, 0)),
                  pl.BlockSpec((HALO, D), lambda i, j: (jnp.maximum(i * hb - 1, 0), 0)),
                  pl.BlockSpec((1, 1, D), lambda i, j: (i // tpb, 0, 3)),
                  pl.BlockSpec((1, 1, D), lambda i, j: (i // tpb, 0, 4)),
                  pl.BlockSpec((1, 1, D), lambda i, j: (i // tpb, 0, 5)),
                  pl.BlockSpec((D, tf), lambda i, j: (0, j)),
                  pl.BlockSpec((D, tf), lambda i, j: (0, nf + j)),
                  pl.BlockSpec((CONV_W, tf), lambda i, j: (0, j)),
                  pl.BlockSpec((CONV_W, tf), lambda i, j: (0, nf + j)),
                  pl.BlockSpec((1, tf), lambda i, j: (0, j)),
                  pl.BlockSpec((1, tf), lambda i, j: (0, nf + j)),
                  pl.BlockSpec((tf, D), lambda i, j: (j, 0))],
        out_specs=pl.BlockSpec((tm, D), lambda i, j: (i, 0), pipeline_mode=pl.Buffered(1)),
        out_shape=jax.ShapeDtypeStruct((T, D), F32),
        scratch_shapes=[pltpu.VMEM((HALO + tm, D), BF16)],
        compiler_params=_cp("parallel", "arbitrary"),
        name="conv_ffn",
    )(x2, x2, mod_l, mod_l, mod_l, w_up, w_up, conv_w, conv_w, conv_b, conv_b, w_down)


def _split3(v):
    a = v.astype(BF16).astype(F32)
    b = (v - a).astype(BF16).astype(F32)
    c = (v - a - b).astype(BF16).astype(F32)
    return a, b, c


def kernel(x, c, positions, w_ada, b_ada, w_in, da_q_gain, da_k_gain, da_lq1, da_lk1, da_lq2, da_lk2, da_head_gain, mla_q_a_gain, mla_w_uq, mla_kv_a_gain, mla_w_ukv, mla_q_gain, mla_k_gain, sg_v_gain, sg_w, sg_b, w_out, ffn_w_up, ffn_conv_w, ffn_conv_b, ffn_w_down):
    B, S, D = x.shape
    L = w_ada.shape[0]
    T = B * S
    assert S % ROW_TILE == 0 and S % KEY_TILE == 0
    tm_big = min(S, FFN_ROWS)
    tm = ROW_TILE
    tq_mla = min(S, MLA_QUERIES)

    mod = _ada_mod(c, w_ada, b_ada)

    d = (positions - positions[:, :1]).astype(F32)
    d_col = d.reshape(T, 1)
    d_row = d.reshape(B, 1, S)
    tabs = _rope_tables(positions.astype(F32).reshape(T, 1))

    H = DA_HEADS
    sl = jnp.asarray(ALIBI_SLOPES, F32) * LOG2E
    slope6 = jnp.tile(jnp.stack(_split3(sl), axis=-1), (1, 2))
    zeros = lambda n: jnp.zeros((H, n), F32)
    kc = jnp.concatenate([zeros(6), slope6, jnp.ones((H, 3), F32), zeros(LANES - 15)], axis=-1)
    kc = kc.reshape(H, 1, LANES)
    sl2 = jnp.broadcast_to((2.0 * sl)[:, None, None], (H, 1, LANES))

    da_scale = DA_QK ** -0.5 * LOG2E
    mla_width = MLA_NOPE + MLA_ROPE
    mla_scale = mla_width ** -0.5 * LOG2E
    pad_r = lambda g: jnp.pad(g, (0, LANES - MLA_ROPE)).reshape(1, LANES)
    shift_lanes = lambda v: jnp.pad(v, (MLA_ROPE, LANES - MLA_ROPE - 3)).reshape(1, LANES)
    kb_mla = shift_lanes(jnp.ones((3,), F32))
    gmax = lambda g: jnp.max(jnp.abs(g))

    w_in_p = jnp.concatenate(
        [w_in[..., 2304:2816], w_in[..., 3136:3648], w_in[..., 3648:4160], w_in[..., 0:768],
         w_in[..., 768:1536], w_in[..., 1536:2304], w_in[..., 2816:3072], w_in[..., 3072:3136],
         jnp.zeros((L, D, IN_PAD - C_KR - MLA_ROPE), w_in.dtype)], axis=-1).astype(BF16)
    wq = mla_w_uq.reshape(L, MLA_Q_RANK, MLA_HEADS, mla_width)
    wqn_all = wq[..., :MLA_NOPE].reshape(L, MLA_Q_RANK, MLA_WIDTH).astype(BF16)
    wqr_all = jnp.pad(wq[..., MLA_NOPE:], ((0, 0), (0, 0), (0, 0), (0, LANES - MLA_ROPE))
                      ).reshape(L, MLA_Q_RANK, MLA_HEADS * LANES).astype(BF16)
    wkv = mla_w_ukv.reshape(L, MLA_KV_RANK, MLA_HEADS, MLA_NOPE + MLA_V)
    wkn_all = wkv[..., :MLA_NOPE].reshape(L, MLA_KV_RANK, MLA_WIDTH).astype(BF16)
    wv_all = wkv[..., MLA_NOPE:].reshape(L, MLA_KV_RANK, MLA_WIDTH).astype(BF16)
    w_out_b, w_up_b, w_down_b = w_out.astype(BF16), ffn_w_up.astype(BF16), ffn_w_down.astype(BF16)

    x2 = x.reshape(T, D)
    for l in range(L):
        mod_l = mod[l].reshape(B, 1, 6 * D)
        z = _norm_matmul(x2, mod_l, 0, w_in_p[l], S, tm, IN_PAD)

        lam_init = 0.8 - 0.6 * math.exp(-0.3 * l)
        gq = (jnp.tile(da_q_gain[l], 2) * da_scale).reshape(1, LANES)
        gk = jnp.tile(da_k_gain[l], 2).reshape(1, LANES)
        bound = BOUND_SLACK * DA_QK * da_scale * gmax(da_q_gain[l]) * gmax(da_k_gain[l])
        nb = jnp.broadcast_to(-jnp.stack(_split3(bound)), (H, 3))
        qc = jnp.concatenate([slope6, zeros(6), nb, zeros(LANES - 15)], axis=-1).reshape(H, 1, LANES)
        q_da, k_da, vt_da = _da_prep(z, d_col, gq, gk, qc, kc, B, S, tm)
        lam_vecs = jnp.stack([da_lq1[l], da_lk1[l], da_lq2[l], da_lk2[l]])
        da_args = (q_da, k_da, vt_da, d_col, d_row, sl2, lam_vecs, da_head_gain[l].reshape(H, 1, DA_V))
        out_a = lax.cond(
            2.0 * bound <= MAX_SHIFT,
            lambda a: _da_attention(*a, B, S, KEY_TILE, lam_init, online=False),
            lambda a: _da_attention(*a, B, S, KEY_TILE, lam_init, online=True), da_args)

        qg, kg = mla_q_gain[l], mla_k_gain[l]
        bound_m = BOUND_SLACK * mla_width * mla_scale * gmax(qg) * gmax(kg)
        q_m, k_m, v_m = _mla_prep(
            z, mla_q_a_gain[l].reshape(1, -1), mla_kv_a_gain[l].reshape(1, -1),
            wqn_all[l], wqr_all[l], wkn_all[l], wv_all[l],
            (qg[:MLA_NOPE] * mla_scale).reshape(1, LANES), pad_r(qg[MLA_NOPE:] * mla_scale),
            kg[:MLA_NOPE].reshape(1, LANES), pad_r(kg[MLA_NOPE:]),
            shift_lanes(-jnp.stack(_split3(bound_m))), kb_mla, tabs, B, S, tm)
        out_b = lax.cond(
            2.0 * bound_m <= MAX_SHIFT,
            lambda a: _mla_attention(*a, B, S, tq_mla, online=False),
            lambda a: _mla_attention(*a, B, S, tq_mla, online=True), (q_m, k_m, v_m))

        sg_bias = jnp.repeat(sg_b[l].T, SG_CH, axis=1)
        out_c = _sg(z, sg_v_gain[l].reshape(1, SG_WIDTH), sg_w[l], sg_bias, tm)

        x2 = _out_proj(out_a, out_b, out_c, w_out_b[l], x2, mod_l, S, tm, D)
        x2 = _ffn(x2, mod_l, w_up_b[l], ffn_conv_w[l], ffn_conv_b[l].reshape(1, -1), w_down_b[l],
                  S, tm_big, FFN_TILE)
    return x2.reshape(B, S, D)
```

```python
import functools
import math

import jax
import jax.numpy as jnp
from jax import lax
from jax.experimental import pallas as pl
from jax.experimental.pallas import tpu as pltpu

CHUNK = 64
EPS = 1e-6
DA_HEADS = 6
DA_QK = 64
DA_V = 128
MLA_HEADS = 6
MLA_Q_RANK = 512
MLA_KV_RANK = 256
MLA_NOPE = 128
MLA_ROPE = 64
MLA_V = 128
ROPE_THETA = 10000.0
SG_GROUPS = 4
SG_CH = 128
SG_LEN = 128
CONV_W = 3
DA_WIDTH = DA_HEADS * DA_V
MLA_WIDTH = MLA_HEADS * MLA_V
SG_WIDTH = SG_GROUPS * SG_CH
ALIBI_SLOPES = tuple(2.0 ** (-8.0 * (h + 1) / DA_HEADS) for h in range(DA_HEADS))

LOG2E = 1.4426950408889634
LANES = 128
QK_PAD = 256
NEG_BIG = -1e30
MAX_SHIFT = 100.0
BOUND_SLACK = 1.02
KV_UNROLL = 4
KEY_TILE = 512
FFN_TILE = 512
FFN_ROWS = 1024
ROW_TILE = 512
MLA_QUERIES = 1024
HALO = 16
NORM_ROWS = 256
VMEM_LIMIT = 56 * 1024 * 1024

C_QA, C_SGU, C_SGV, C_DAQ, C_DAK, C_DAV, C_KVA, C_KR, IN_PAD = (
    0, 512, 1024, 1536, 2304, 3072, 3840, 4096, 4224)

BF16 = jnp.bfloat16
F32 = jnp.float32


def _cp(*sem):
    return pltpu.CompilerParams(dimension_semantics=sem, vmem_limit_bytes=VMEM_LIMIT)


def _dot(a, b):
    return jnp.dot(a, b, preferred_element_type=F32)


def _dot_nt(a, b):
    return lax.dot_general(a, b, (((1,), (1,)), ((), ())), preferred_element_type=F32)


def _rms_scale(x, n):
    return lax.rsqrt(jnp.sum(x * x, axis=-1, keepdims=True) * (1.0 / n) + EPS)


def _same_group_matrix(group):
    r = lax.broadcasted_iota(jnp.int32, (LANES, LANES), 0)
    c = lax.broadcasted_iota(jnp.int32, (LANES, LANES), 1)
    shift = group.bit_length() - 1
    same = lax.shift_right_logical(r, shift) == lax.shift_right_logical(c, shift)
    return jnp.where(same, 1.0, 0.0).astype(BF16)


def _relayout_kernel(src_ref, *dst_refs, plans):
    x = src_ref[0]
    for dst_ref, moves in zip(dst_refs, plans):
        covered = 0
        for d0, s0, w in sorted(moves):
            if d0 > covered:
                dst_ref[0, :, covered:d0] = jnp.zeros((x.shape[0], d0 - covered), BF16)
            dst_ref[0, :, d0:d0 + w] = x[:, s0:s0 + w].astype(BF16)
            covered = d0 + w
        width = dst_ref.shape[-1]
        if covered < width:
            dst_ref[0, :, covered:] = jnp.zeros((x.shape[0], width - covered), BF16)


def _relayout(w, plans, widths, rows):
    L, R, C = w.shape
    return pl.pallas_call(
        functools.partial(_relayout_kernel, plans=plans),
        grid=(L, R // rows),
        in_specs=[pl.BlockSpec((1, rows, C), lambda l, i: (l, i, 0))],
        out_specs=[pl.BlockSpec((1, rows, n), lambda l, i: (l, i, 0)) for n in widths],
        out_shape=[jax.ShapeDtypeStruct((L, R, n), BF16) for n in widths],
        compiler_params=_cp("parallel", "parallel"),
        name="weight_relayout",
    )(w)


def _row_relayout_kernel(src_ref, dst_ref, *, moves):
    covered = 0
    cols = dst_ref.shape[-1]
    for d0, s0, n in sorted(moves):
        if d0 > covered:
            dst_ref[0, covered:d0, :] = jnp.zeros((d0 - covered, cols), BF16)
        dst_ref[0, d0:d0 + n, :] = src_ref[0, s0:s0 + n, :].astype(BF16)
        covered = d0 + n
    if covered < dst_ref.shape[1]:
        dst_ref[0, covered:, :] = jnp.zeros((dst_ref.shape[1] - covered, cols), BF16)


def _row_relayout(wt, moves, rows_out, cols):
    L, R, C = wt.shape
    return pl.pallas_call(
        functools.partial(_row_relayout_kernel, moves=moves),
        grid=(L, C // cols),
        in_specs=[pl.BlockSpec((1, R, cols), lambda l, i: (l, 0, i))],
        out_specs=pl.BlockSpec((1, rows_out, cols), lambda l, i: (l, 0, i)),
        out_shape=jax.ShapeDtypeStruct((L, rows_out, C), BF16),
        compiler_params=_cp("parallel", "parallel"),
        name="weight_row_relayout",
    )(wt)


def _ada_kernel(ct_ref, w_ref, b_ref, o_ref):
    c = ct_ref[...]
    cond = c / (1.0 + jnp.exp(-c))
    w = w_ref[0]
    for b in range(c.shape[1]):
        o_ref[0, b:b + 1, :] = jnp.sum(w * cond[:, b:b + 1], axis=0, keepdims=True) + b_ref[0]


def _ada_mod(c, w_ada, b_ada):
    L, D, N = w_ada.shape
    B = c.shape[0]
    tn = 1024
    return pl.pallas_call(
        _ada_kernel,
        grid=(L, N // tn),
        in_specs=[pl.BlockSpec((D, B), lambda l, j: (0, 0)),
                  pl.BlockSpec((1, D, tn), lambda l, j: (l, 0, j)),
                  pl.BlockSpec((1, 1, tn), lambda l, j: (l, 0, j))],
        out_specs=pl.BlockSpec((1, B, tn), lambda l, j: (l, 0, j)),
        out_shape=jax.ShapeDtypeStruct((L, B, N), F32),
        compiler_params=_cp("parallel", "parallel"),
        name="ada_mod",
    )(c.T, w_ada, b_ada.reshape(L, 1, N))


def _rope_table_kernel(pos_ref, inv_ref, c_ref, s1_ref, s2_ref):
    ang = pos_ref[...] * inv_ref[...]
    lane = lax.broadcasted_iota(jnp.int32, ang.shape, 1)
    half = MLA_ROPE // 2
    cos = jnp.cos(ang)
    sin = jnp.sin(ang)
    c_ref[...] = jnp.where(lane < MLA_ROPE, cos, 0.0)
    s1_ref[...] = jnp.where(lane < half, -sin, 0.0)
    s2_ref[...] = jnp.where((lane >= half) & (lane < MLA_ROPE), sin, 0.0)


def _rope_tables(pos_col):
    T = pos_col.shape[0]
    half = MLA_ROPE // 2
    inv = ROPE_THETA ** (-jnp.arange(half, dtype=F32) / half)
    inv_row = jnp.concatenate([inv, inv, jnp.zeros((LANES - MLA_ROPE,), F32)]).reshape(1, LANES)
    tm = min(T, 2048)
    sds = jax.ShapeDtypeStruct((T, LANES), F32)
    spec = pl.BlockSpec((tm, LANES), lambda i: (i, 0))
    return pl.pallas_call(
        _rope_table_kernel,
        grid=(T // tm,),
        in_specs=[pl.BlockSpec((tm, 1), lambda i: (i, 0)),
                  pl.BlockSpec((1, LANES), lambda i: (0, 0))],
        out_specs=[spec, spec, spec],
        out_shape=[sds, sds, sds],
        compiler_params=_cp("parallel"),
        name="rope_tables",
    )(pos_col, inv_row)


def _norm_matmul_kernel(x_ref, sh_ref, sc_ref, w_ref, o_ref, h_scr):
    @pl.when(pl.program_id(1) == 0)
    def _():
        for r in range(0, x_ref.shape[0], NORM_ROWS):
            x = x_ref[r:r + NORM_ROWS, :]
            h = x * _rms_scale(x, x.shape[-1]) * (1.0 + sc_ref[0]) + sh_ref[0]
            h_scr[r:r + NORM_ROWS, :] = h.astype(BF16)

    o_ref[...] = _dot_nt(h_scr[...], w_ref[...]).astype(o_ref.dtype)


def _norm_matmul(x2, mod_l, which, wt, l, S, tm, tn):
    T, D = x2.shape
    N = wt.shape[1]
    tpb = S // tm
    w_mode = dict(pipeline_mode=pl.Buffered(1)) if tn == N else {}
    return pl.pallas_call(
        _norm_matmul_kernel,
        grid=(T // tm, N // tn),
        in_specs=[pl.BlockSpec((tm, D), lambda i, j: (i, 0)),
                  pl.BlockSpec((1, 1, D), lambda i, j: (i // tpb, 0, which)),
                  pl.BlockSpec((1, 1, D), lambda i, j: (i // tpb, 0, which + 1)),
                  pl.BlockSpec((None, tn, D), lambda i, j: (l, j, 0), **w_mode)],
        out_specs=pl.BlockSpec((tm, tn), lambda i, j: (i, j)),
        out_shape=jax.ShapeDtypeStruct((T, N), BF16),
        scratch_shapes=[pltpu.VMEM((tm, D), BF16)],
        compiler_params=_cp("parallel", "arbitrary"),
        name="norm_matmul",
    )(x2, mod_l, mod_l, wt)


def _da_prep_kernel(zq_ref, zk_ref, zv_ref, d_ref, gq_ref, gk_ref, qc_ref, kc_ref, q_ref, k_ref, vt_ref):
    lane = lax.broadcasted_iota(jnp.int32, (1, LANES), 1)
    lo = lane < DA_QK
    same_map = _same_group_matrix(DA_QK)

    def qk_norm(z, g):
        ss = _dot((z * z).astype(BF16), same_map)
        return z * lax.rsqrt(ss * (1.0 / DA_QK) + EPS) * g

    d = d_ref[...]
    d_hi = jnp.floor(d * (1.0 / LANES)) * LANES
    d_lo = d - d_hi
    k_pos = jnp.where(lane < 3, d_hi, jnp.where(lane < 6, d_lo, 0.0))
    q_pos = jnp.where((lane >= 6) & (lane < 9), -d_hi, jnp.where((lane >= 9) & (lane < 12), -d_lo, 0.0))
    for h in range(DA_HEADS):
        sl = slice(h * LANES, (h + 1) * LANES)
        qn = qk_norm(zq_ref[:, sl].astype(F32), gq_ref[...])
        kn = qk_norm(zk_ref[:, sl].astype(F32), gk_ref[...])
        qa = (q_pos + qc_ref[h]).astype(BF16)
        q_ref[0, h, 0, :, :LANES] = jnp.where(lo, qn, 0.0).astype(BF16)
        q_ref[0, h, 0, :, LANES:] = qa
        q_ref[0, h, 1, :, :LANES] = jnp.where(lo, 0.0, qn).astype(BF16)
        q_ref[0, h, 1, :, LANES:] = qa
        k_ref[0, h, :, :LANES] = kn.astype(BF16)
        k_ref[0, h, :, LANES:] = (k_pos + kc_ref[h]).astype(BF16)
        vt_ref[0, h] = zv_ref[:, sl].astype(F32).T.astype(BF16)


def _da_prep(z, d_col, gq, gk, qc, kc, B, S, tm):
    T = z.shape[0]
    tpb = S // tm
    H = DA_HEADS
    return pl.pallas_call(
        _da_prep_kernel,
        grid=(T // tm,),
        in_specs=[pl.BlockSpec((tm, DA_WIDTH), lambda i: (i, C_DAQ // DA_WIDTH)),
                  pl.BlockSpec((tm, DA_WIDTH), lambda i: (i, C_DAK // DA_WIDTH)),
                  pl.BlockSpec((tm, DA_WIDTH), lambda i: (i, C_DAV // DA_WIDTH)),
                  pl.BlockSpec((tm, 1), lambda i: (i, 0)),
                  pl.BlockSpec((1, LANES), lambda i: (0, 0)),
                  pl.BlockSpec((1, LANES), lambda i: (0, 0)),
                  pl.BlockSpec((H, 1, LANES), lambda i: (0, 0, 0)),
                  pl.BlockSpec((H, 1, LANES), lambda i: (0, 0, 0))],
        out_specs=[pl.BlockSpec((1, H, 2, tm, QK_PAD), lambda i: (i // tpb, 0, 0, i % tpb, 0)),
                   pl.BlockSpec((1, H, tm, QK_PAD), lambda i: (i // tpb, 0, i % tpb, 0)),
                   pl.BlockSpec((1, H, DA_V, tm), lambda i: (i // tpb, 0, 0, i % tpb))],
        out_shape=[jax.ShapeDtypeStruct((B, H, 2, S, QK_PAD), BF16),
                   jax.ShapeDtypeStruct((B, H, S, QK_PAD), BF16),
                   jax.ShapeDtypeStruct((B, H, DA_V, S), BF16)],
        compiler_params=_cp("parallel"),
        name="da_prep",
    )(z, z, z, d_col, gq, gk, qc, kc)


def _mla_prep_kernel(zqa_ref, zkva_ref, zkr_ref, gqa_ref, gkva_ref, wqn_ref, wqr_ref, wkn_ref, wv_ref,
                     gqn_ref, gqr_ref, gkn_ref, gkr_ref, qb_ref, kb_ref, c_ref, s1_ref, s2_ref,
                     q_ref, k_ref, v_ref):
    half = MLA_ROPE // 2
    width = MLA_NOPE + MLA_ROPE
    cos, sin1, sin2 = c_ref[...], s1_ref[...], s2_ref[...]

    def rope(t):
        return (t * cos + pltpu.roll(t, LANES - half, 1) * sin1 + pltpu.roll(t, half, 1) * sin2)

    zqa = zqa_ref[...].astype(F32)
    qa = (zqa * _rms_scale(zqa, MLA_Q_RANK) * gqa_ref[...]).astype(BF16)
    zkva = zkva_ref[...].astype(F32)
    kva = (zkva * _rms_scale(zkva, MLA_KV_RANK) * gkva_ref[...]).astype(BF16)
    qn_all = _dot(qa, wqn_ref[...])
    qr_all = _dot(qa, wqr_ref[...])
    kn_all = _dot(kva, wkn_ref[...])
    v_all = _dot(kva, wv_ref[...])
    kr = zkr_ref[...].astype(F32)
    kr_sq = kr * kr
    kr_rot = rope(kr * gkr_ref[...])
    ones = _same_group_matrix(LANES)
    for h in range(MLA_HEADS):
        sl = slice(h * LANES, (h + 1) * LANES)
        qn, qr, kn = qn_all[:, sl], qr_all[:, sl], kn_all[:, sl]
        q_inv = lax.rsqrt(_dot((qn * qn + qr * qr).astype(BF16), ones) * (1.0 / width) + EPS)
        k_inv = lax.rsqrt(_dot((kn * kn + kr_sq).astype(BF16), ones) * (1.0 / width) + EPS)
        q_ref[0, h, :, :LANES] = (qn * q_inv * gqn_ref[...]).astype(BF16)
        q_ref[0, h, :, LANES:] = (rope(qr * q_inv * gqr_ref[...]) + qb_ref[...]).astype(BF16)
        k_ref[0, h, :, :LANES] = (kn * k_inv * gkn_ref[...]).astype(BF16)
        k_ref[0, h, :, LANES:] = (kr_rot * k_inv + kb_ref[...]).astype(BF16)
        v_ref[0, h] = v_all[:, sl].T.astype(BF16)


def _mla_prep(z, gqa, gkva, wqn, wqr, wkn, wv, l, gqn, gqr, gkn, gkr, qb, kb, tabs, B, S, tm):
    T = z.shape[0]
    tpb = S // tm
    H = MLA_HEADS
    full = lambda a: pl.BlockSpec(a.shape, lambda i: (0,) * a.ndim)
    layer = lambda a: pl.BlockSpec((None,) + a.shape[1:], lambda i: (l, 0, 0))
    row = pl.BlockSpec((tm, LANES), lambda i: (i, 0))
    return pl.pallas_call(
        _mla_prep_kernel,
        grid=(T // tm,),
        in_specs=[pl.BlockSpec((tm, MLA_Q_RANK), lambda i: (i, C_QA // MLA_Q_RANK)),
                  pl.BlockSpec((tm, MLA_KV_RANK), lambda i: (i, C_KVA // MLA_KV_RANK)),
                  pl.BlockSpec((tm, LANES), lambda i: (i, C_KR // LANES)),
                  full(gqa), full(gkva), layer(wqn), layer(wqr), layer(wkn), layer(wv),
                  full(gqn), full(gqr), full(gkn), full(gkr), full(qb), full(kb), row, row, row],
        out_specs=[pl.BlockSpec((1, H, tm, QK_PAD), lambda i: (i // tpb, 0, i % tpb, 0)),
                   pl.BlockSpec((1, H, tm, QK_PAD), lambda i: (i // tpb, 0, i % tpb, 0)),
                   pl.BlockSpec((1, H, MLA_V, tm), lambda i: (i // tpb, 0, 0, i % tpb))],
        out_shape=[jax.ShapeDtypeStruct((B, H, S, QK_PAD), BF16),
                   jax.ShapeDtypeStruct((B, H, S, QK_PAD), BF16),
                   jax.ShapeDtypeStruct((B, H, MLA_V, S), BF16)],
        compiler_params=_cp("parallel"),
        name="mla_prep",
    )(z, z, z, gqa, gkva, wqn, wqr, wkn, wv, gqn, gqr, gkn, gkr, qb, kb, *tabs)


def _gelu_tanh(x):
    return 0.5 * x * (1.0 + jnp.tanh(0.7978845608028654 * (x + 0.044715 * (x * x * x))))


def _sg_kernel(zu_ref, zv_ref, g_ref, w_ref, b_ref, o_ref):
    tm = zu_ref.shape[0]
    r = lax.broadcasted_iota(jnp.int32, (SG_LEN, SG_LEN), 0)
    c = lax.broadcasted_iota(jnp.int32, (SG_LEN, SG_LEN), 1)
    causal = c <= r
    for g in range(SG_GROUPS):
        sl = slice(g * SG_CH, (g + 1) * SG_CH)
        wt = jnp.where(causal, w_ref[g], 0.0).astype(BF16)
        v = _gelu_tanh(zv_ref[:, sl].astype(F32))
        v = (v * _rms_scale(v, SG_CH) * g_ref[:, sl]).astype(BF16)
        u = _gelu_tanh(zu_ref[:, sl].astype(F32))
        for n in range(tm // SG_LEN):
            rows = slice(n * SG_LEN, (n + 1) * SG_LEN)
            s = _dot(wt, v[rows, :]) + b_ref[:, sl]
            o_ref[rows, sl] = (u[rows, :] * s).astype(BF16)


def _sg(z, gain, w, bias, tm):
    T = z.shape[0]
    return pl.pallas_call(
        _sg_kernel,
        grid=(T // tm,),
        in_specs=[pl.BlockSpec((tm, SG_WIDTH), lambda i: (i, C_SGU // SG_WIDTH)),
                  pl.BlockSpec((tm, SG_WIDTH), lambda i: (i, C_SGV // SG_WIDTH)),
                  pl.BlockSpec((1, SG_WIDTH), lambda i: (0, 0)),
                  pl.BlockSpec((SG_GROUPS, SG_LEN, SG_LEN), lambda i: (0, 0, 0)),
                  pl.BlockSpec((SG_LEN, SG_WIDTH), lambda i: (0, 0))],
        out_specs=pl.BlockSpec((tm, SG_WIDTH), lambda i: (i, 0)),
        out_shape=jax.ShapeDtypeStruct((T, SG_WIDTH), BF16),
        compiler_params=_cp("parallel"),
        name="spatial_gating",
    )(z, z, gain, w, bias)


def _flash_t(q, k_ref, vt_ref, n_off, diag_fixes, online):
    n = q.shape[0]
    qt = q.astype(F32).T.astype(BF16)

    def step(j, state, fix=None):
        m, l, acc = state
        start = pl.multiple_of(j * KEY_TILE, KEY_TILE)
        s = _dot(k_ref[0, 0, pl.ds(start, KEY_TILE), :], qt)
        if fix is not None:
            s = fix(s)
        if online:
            m_new = jnp.maximum(m, jnp.max(s, axis=0, keepdims=True))
            alpha = jnp.exp2(m - m_new)
            p = jnp.exp2(s - m_new)
            l = alpha * l
            acc = alpha * acc
        else:
            m_new = m
            p = jnp.exp2(s)
        l = l + jnp.sum(p, axis=0, keepdims=True)
        acc = acc + _dot(vt_ref[0, 0, :, pl.ds(start, KEY_TILE)], p.astype(BF16))
        return m_new, l, acc

    def run(j0, count, state):
        for u in range(count):
            state = step(j0 + u, state)
        return state

    state = (jnp.full((1, n), NEG_BIG, F32), jnp.zeros((1, n), F32), jnp.zeros((DA_V, n), F32))
    n_grp = n_off // KV_UNROLL
    state = lax.fori_loop(0, n_grp, lambda g, st: run(g * KV_UNROLL, KV_UNROLL, st), state)
    state = lax.fori_loop(n_grp * KV_UNROLL, n_off, lambda j, st: run(j, 1, st), state)
    for t, fix in enumerate(diag_fixes):
        state = step(n_off + t, state, fix)
    _, l, acc = state
    return acc / l


def _chunk_visible(n, tq, key_off):
    r = lax.broadcasted_iota(jnp.int32, (KEY_TILE, n), 0) + key_off
    c = lax.broadcasted_iota(jnp.int32, (KEY_TILE, n), 1)
    if n > tq:
        c = jnp.where(c >= tq, c - tq, c)
    shift = CHUNK.bit_length() - 1
    return lax.shift_right_logical(r, shift) <= lax.shift_right_logical(c, shift)


def _da_attn_kernel(q_ref, k_ref, vt_ref, dk_ref, dq_ref, sl2_ref, lam_ref, hg_ref, o_ref, *,
                    tq, lam_init, online):
    qi = pl.program_id(2)
    q = q_ref[0, 0].reshape(2 * tq, QK_PAD)
    def fix(s):
        dq = dq_ref[0]
        dq2 = jnp.concatenate([dq, dq], axis=1)
        corr = sl2_ref[0, :, :1] * jnp.maximum(dk_ref[...] - dq2, 0.0)
        return jnp.where(_chunk_visible(2 * tq, tq, 0), s - corr, NEG_BIG)

    ot = _flash_t(q, k_ref, vt_ref, qi, [fix], online)
    lv = lam_ref[...]
    lam = (jnp.exp(jnp.sum(lv[0:1] * lv[1:2], axis=-1, keepdims=True))
           - jnp.exp(jnp.sum(lv[2:3] * lv[3:4], axis=-1, keepdims=True)) + lam_init)
    od = ot[:, :tq] - lam * ot[:, tq:]
    inv = lax.rsqrt(jnp.sum(od * od, axis=0, keepdims=True) * (1.0 / DA_V) + EPS)
    od = od * inv * (1.0 - lam_init)
    o_ref[...] = (od.T * hg_ref[0]).astype(o_ref.dtype)


def _da_attention(q, k, vt, d_col, d_row, sl2, lam_vecs, hg, B, S, tq, lam_init, online):
    H = DA_HEADS
    nq = S // tq
    assert tq == KEY_TILE
    return pl.pallas_call(
        functools.partial(_da_attn_kernel, tq=tq, lam_init=lam_init, online=online),
        grid=(B, H, nq),
        in_specs=[pl.BlockSpec((1, 1, 2, tq, QK_PAD), lambda b, h, i: (b, h, 0, i, 0)),
                  pl.BlockSpec((1, 1, S, QK_PAD), lambda b, h, i: (b, h, 0, 0)),
                  pl.BlockSpec((1, 1, DA_V, S), lambda b, h, i: (b, h, 0, 0)),
                  pl.BlockSpec((tq, 1), lambda b, h, i: (b * nq + i, 0)),
                  pl.BlockSpec((1, 1, tq), lambda b, h, i: (b, 0, i)),
                  pl.BlockSpec((1, 1, LANES), lambda b, h, i: (h, 0, 0)),
                  pl.BlockSpec((4, DA_QK), lambda b, h, i: (0, 0)),
                  pl.BlockSpec((1, 1, DA_V), lambda b, h, i: (h, 0, 0))],
        out_specs=pl.BlockSpec((tq, DA_V), lambda b, h, i: (b * nq + i, h)),
        out_shape=jax.ShapeDtypeStruct((B * S, DA_WIDTH), BF16),
        compiler_params=_cp("parallel", "parallel", "parallel"),
        name="da_attention",
    )(q, k, vt, d_col, d_row, sl2, lam_vecs, hg)


def _mla_attn_kernel(q_ref, k_ref, vt_ref, o_ref, *, tq, online):
    qi = pl.program_id(2)
    tiles = tq // KEY_TILE
    fixes = [lambda s, t=t: jnp.where(_chunk_visible(tq, tq, t * KEY_TILE), s, NEG_BIG) for t in range(tiles)]
    ot = _flash_t(q_ref[0, 0], k_ref, vt_ref, qi * tiles, fixes, online)
    o_ref[...] = ot.T.astype(o_ref.dtype)


def _mla_attention(q, k, vt, B, S, tq, online):
    H = MLA_HEADS
    nq = S // tq
    return pl.pallas_call(
        functools.partial(_mla_attn_kernel, tq=tq, online=online),
        grid=(B, H, nq),
        in_specs=[pl.BlockSpec((1, 1, tq, QK_PAD), lambda b, h, i: (b, h, i, 0)),
                  pl.BlockSpec((1, 1, S, QK_PAD), lambda b, h, i: (b, h, 0, 0)),
                  pl.BlockSpec((1, 1, MLA_V, S), lambda b, h, i: (b, h, 0, 0))],
        out_specs=pl.BlockSpec((tq, MLA_V), lambda b, h, i: (b * nq + i, h)),
        out_shape=jax.ShapeDtypeStruct((B * S, MLA_WIDTH), BF16),
        compiler_params=_cp("parallel", "parallel", "parallel"),
        name="mla_attention",
    )(q, k, vt)


def _out_proj_kernel(a_ref, b_ref, c_ref, w_ref, x_ref, g_ref, o_ref):
    ka, kb = a_ref.shape[1], a_ref.shape[1] + b_ref.shape[1]
    mix = (_dot(a_ref[...], w_ref[:ka, :].astype(BF16)) + _dot(b_ref[...], w_ref[ka:kb, :].astype(BF16))
           + _dot(c_ref[...], w_ref[kb:, :].astype(BF16)))
    o_ref[...] = x_ref[...] + g_ref[0] * mix


def _out_proj(a, b, c, w, l, x2, mod_l, S, tm, tn):
    T, D = x2.shape
    tpb = S // tm
    nb = D // tn
    w_mode = dict(pipeline_mode=pl.Buffered(1)) if nb == 1 else {}
    return pl.pallas_call(
        _out_proj_kernel,
        grid=(T // tm, nb),
        in_specs=[pl.BlockSpec((tm, DA_WIDTH), lambda i, j: (i, 0)),
                  pl.BlockSpec((tm, MLA_WIDTH), lambda i, j: (i, 0)),
                  pl.BlockSpec((tm, SG_WIDTH), lambda i, j: (i, 0)),
                  pl.BlockSpec((None, w.shape[1], tn), lambda i, j: (l, 0, j), **w_mode),
                  pl.BlockSpec((tm, tn), lambda i, j: (i, j)),
                  pl.BlockSpec((1, 1, tn), lambda i, j: (i // tpb, 0, 2 * nb + j))],
        out_specs=pl.BlockSpec((tm, tn), lambda i, j: (i, j)),
        out_shape=jax.ShapeDtypeStruct((T, D), F32),
        compiler_params=_cp("parallel", "parallel"),
        name="out_proj",
    )(a, b, c, w, x2, mod_l)


def _ffn_kernel(x_ref, xh_ref, sh_ref, sc_ref, g_ref, wg_ref, wv_ref, cwg_ref, cwv_ref, cbg_ref, cbv_ref,
                wd_ref, o_ref, h_scr, *, tpb):
    i = pl.program_id(0)
    j = pl.program_id(1)

    @pl.when(j == 0)
    def _():
        def modnorm(x):
            return x * _rms_scale(x, x.shape[-1]) * (1.0 + sc_ref[0]) + sh_ref[0]

        for r in range(0, x_ref.shape[0], NORM_ROWS):
            h_scr[HALO + r:HALO + r + NORM_ROWS, :] = modnorm(x_ref[r:r + NORM_ROWS, :]).astype(BF16)
        keep = jnp.where(i % tpb == 0, 0.0, 1.0)
        h_scr[:HALO, :] = (modnorm(xh_ref[...]) * keep).astype(BF16)
        o_ref[...] = jnp.zeros_like(o_ref)

    h = h_scr[...]

    def conv(a, cw_ref, cb_ref):
        y = (cb_ref[...] + a * cw_ref[2:3, :] + pltpu.roll(a, 1, 0) * cw_ref[1:2, :]
             + pltpu.roll(a, 2, 0) * cw_ref[0:1, :])
        return y[HALO:, :]

    yg = conv(_dot(h, wg_ref[...]), cwg_ref, cbg_ref)
    yv = conv(_dot(h, wv_ref[...]), cwv_ref, cbv_ref)
    act = (yg / (1.0 + jnp.exp(-yg)) * yv).astype(BF16)
    o_ref[...] += _dot(act, wd_ref[...].astype(BF16))

    @pl.when(j == pl.num_programs(1) - 1)
    def _():
        o_ref[...] = x_ref[...] + g_ref[0] * o_ref[...]


def _ffn(x2, mod_l, w_up, conv_w, conv_b, w_down, l, S, tm, tf):
    T, D = x2.shape
    F = w_down.shape[1]
    nf = F // tf
    tpb = S // tm
    hb = tm // HALO
    return pl.pallas_call(
        functools.partial(_ffn_kernel, tpb=tpb),
        grid=(T // tm, nf),
        in_specs=[pl.BlockSpec((tm, D), lambda i, j: (i, 0)),
                  pl.BlockSpec((HALO, D), lambda i, j: (jnp.maximum(i * hb - 1, 0), 0)),
                  pl.BlockSpec((1, 1, D), lambda i, j: (i // tpb, 0, 3)),
                  pl.BlockSpec((1, 1, D), lambda i, j: (i // tpb, 0, 4)),
                  pl.BlockSpec((1, 1, D), lambda i, j: (i // tpb, 0, 5)),
                  pl.BlockSpec((None, D, tf), lambda i, j: (l, 0, j)),
                  pl.BlockSpec((None, D, tf), lambda i, j: (l, 0, nf + j)),
                  pl.BlockSpec((None, CONV_W, tf), lambda i, j: (l, 0, j)),
                  pl.BlockSpec((None, CONV_W, tf), lambda i, j: (l, 0, nf + j)),
                  pl.BlockSpec((None, 1, tf), lambda i, j: (l, 0, j)),
                  pl.BlockSpec((None, 1, tf), lambda i, j: (l, 0, nf + j)),
                  pl.BlockSpec((None, tf, D), lambda i, j: (l, j, 0))],
        out_specs=pl.BlockSpec((tm, D), lambda i, j: (i, 0), pipeline_mode=pl.Buffered(1)),
        out_shape=jax.ShapeDtypeStruct((T, D), F32),
        scratch_shapes=[pltpu.VMEM((HALO + tm, D), BF16)],
        compiler_params=_cp("parallel", "arbitrary"),
        name="conv_ffn",
    )(x2, x2, mod_l, mod_l, mod_l, w_up, w_up, conv_w, conv_w, conv_b, conv_b, w_down)


def _split3(v):
    a = v.astype(BF16).astype(F32)
    b = (v - a).astype(BF16).astype(F32)
    c = (v - a - b).astype(BF16).astype(F32)
    return a, b, c


def kernel(x, c, positions, w_ada, b_ada, w_in, da_q_gain, da_k_gain, da_lq1, da_lk1, da_lq2, da_lk2, da_head_gain, mla_q_a_gain, mla_w_uq, mla_kv_a_gain, mla_w_ukv, mla_q_gain, mla_k_gain, sg_v_gain, sg_w, sg_b, w_out, ffn_w_up, ffn_conv_w, ffn_conv_b, ffn_w_down):
    B, S, D = x.shape
    L = w_ada.shape[0]
    T = B * S
    assert S % ROW_TILE == 0 and S % KEY_TILE == 0
    tm_big = min(S, FFN_ROWS)
    tm = ROW_TILE
    tq_mla = min(S, MLA_QUERIES)

    mod = _ada_mod(c, w_ada, b_ada)

    d = (positions - positions[:, :1]).astype(F32)
    d_col = d.reshape(T, 1)
    d_row = d.reshape(B, 1, S)
    tabs = _rope_tables(positions.astype(F32).reshape(T, 1))

    H = DA_HEADS
    sl = jnp.asarray(ALIBI_SLOPES, F32) * LOG2E
    slope6 = jnp.tile(jnp.stack(_split3(sl), axis=-1), (1, 2))
    zeros = lambda n: jnp.zeros((H, n), F32)
    kc = jnp.concatenate([zeros(6), slope6, jnp.ones((H, 3), F32), zeros(LANES - 15)], axis=-1)
    kc = kc.reshape(H, 1, LANES)
    sl2 = jnp.broadcast_to((2.0 * sl)[:, None, None], (H, 1, LANES))

    da_scale = DA_QK ** -0.5 * LOG2E
    mla_width = MLA_NOPE + MLA_ROPE
    mla_scale = mla_width ** -0.5 * LOG2E
    pad_r = lambda g: jnp.pad(g, (0, LANES - MLA_ROPE)).reshape(1, LANES)
    shift_lanes = lambda v: jnp.pad(v, (MLA_ROPE, LANES - MLA_ROPE - 3)).reshape(1, LANES)
    kb_mla = shift_lanes(jnp.ones((3,), F32))
    gmax = lambda g: jnp.max(jnp.abs(g))

    in_moves = [(C_QA, 2304, 512), (C_SGU, 3136, 512), (C_SGV, 3648, 512), (C_DAQ, 0, 768),
                (C_DAK, 768, 768), (C_DAV, 1536, 768), (C_KVA, 2816, 256), (C_KR, 3072, MLA_ROPE)]
    w_in_p = _row_relayout(jnp.swapaxes(w_in, 1, 2), in_moves, IN_PAD, 512)
    heads = range(MLA_HEADS)
    wqn_all, wqr_all = _relayout(
        mla_w_uq,
        [[(h * LANES, h * mla_width, MLA_NOPE) for h in heads],
         [(h * LANES, h * mla_width + MLA_NOPE, MLA_ROPE) for h in heads]],
        [MLA_WIDTH, MLA_HEADS * LANES], 256)
    kv_width = MLA_NOPE + MLA_V
    wkn_all, wv_all = _relayout(
        mla_w_ukv,
        [[(h * LANES, h * kv_width, MLA_NOPE) for h in heads],
         [(h * LANES, h * kv_width + MLA_NOPE, MLA_V) for h in heads]],
        [MLA_WIDTH, MLA_WIDTH], 256)
    w_up_b = ffn_w_up.astype(BF16)
    conv_b3 = ffn_conv_b.reshape(L, 1, -1)

    x2 = x.reshape(T, D)
    for l in range(L):
        mod_l = mod[l].reshape(B, 1, 6 * D)
        z = _norm_matmul(x2, mod_l, 0, w_in_p, l, S, tm, IN_PAD)

        lam_init = 0.8 - 0.6 * math.exp(-0.3 * l)
        gq = (jnp.tile(da_q_gain[l], 2) * da_scale).reshape(1, LANES)
        gk = jnp.tile(da_k_gain[l], 2).reshape(1, LANES)
        bound = BOUND_SLACK * DA_QK * da_scale * gmax(da_q_gain[l]) * gmax(da_k_gain[l])
        nb = jnp.broadcast_to(-jnp.stack(_split3(bound)), (H, 3))
        qc = jnp.concatenate([slope6, zeros(6), nb, zeros(LANES - 15)], axis=-1).reshape(H, 1, LANES)
        q_da, k_da, vt_da = _da_prep(z, d_col, gq, gk, qc, kc, B, S, tm)
        lam_vecs = jnp.stack([da_lq1[l], da_lk1[l], da_lq2[l], da_lk2[l]])
        da_args = (q_da, k_da, vt_da, d_col, d_row, sl2, lam_vecs, da_head_gain[l].reshape(H, 1, DA_V))
        out_a = lax.cond(
            2.0 * bound <= MAX_SHIFT,
            lambda a: _da_attention(*a, B, S, KEY_TILE, lam_init, online=False),
            lambda a: _da_attention(*a, B, S, KEY_TILE, lam_init, online=True), da_args)

        qg, kg = mla_q_gain[l], mla_k_gain[l]
        bound_m = BOUND_SLACK * mla_width * mla_scale * gmax(qg) * gmax(kg)
        q_m, k_m, v_m = _mla_prep(
            z, mla_q_a_gain[l].reshape(1, -1), mla_kv_a_gain[l].reshape(1, -1),
            wqn_all, wqr_all, wkn_all, wv_all, l,
            (qg[:MLA_NOPE] * mla_scale).reshape(1, LANES), pad_r(qg[MLA_NOPE:] * mla_scale),
            kg[:MLA_NOPE].reshape(1, LANES), pad_r(kg[MLA_NOPE:]),
            shift_lanes(-jnp.stack(_split3(bound_m))), kb_mla, tabs, B, S, tm)
        out_b = lax.cond(
            2.0 * bound_m <= MAX_SHIFT,
            lambda a: _mla_attention(*a, B, S, tq_mla, online=False),
            lambda a: _mla_attention(*a, B, S, tq_mla, online=True), (q_m, k_m, v_m))

        sg_bias = jnp.repeat(sg_b[l].T, SG_CH, axis=1)
        out_c = _sg(z, sg_v_gain[l].reshape(1, SG_WIDTH), sg_w[l], sg_bias, tm)

        x2 = _out_proj(out_a, out_b, out_c, w_out, l, x2, mod_l, S, tm, D)
        x2 = _ffn(x2, mod_l, w_up_b, ffn_conv_w, conv_b3, ffn_w_down, l, S, tm_big, FFN_TILE)
    return x2.reshape(B, S, D)
```

```python
import functools
import math

import jax
import jax.numpy as jnp
from jax import lax
from jax.experimental import pallas as pl
from jax.experimental.pallas import tpu as pltpu

CHUNK = 64
EPS = 1e-6
DA_HEADS = 6
DA_QK = 64
DA_V = 128
MLA_HEADS = 6
MLA_Q_RANK = 512
MLA_KV_RANK = 256
MLA_NOPE = 128
MLA_ROPE = 64
MLA_V = 128
ROPE_THETA = 10000.0
SG_GROUPS = 4
SG_CH = 128
SG_LEN = 128
CONV_W = 3
DA_WIDTH = DA_HEADS * DA_V
MLA_WIDTH = MLA_HEADS * MLA_V
SG_WIDTH = SG_GROUPS * SG_CH
ALIBI_SLOPES = tuple(2.0 ** (-8.0 * (h + 1) / DA_HEADS) for h in range(DA_HEADS))

LOG2E = 1.4426950408889634
LANES = 128
QK_PAD = 256
NEG_BIG = -1e30
MAX_SHIFT = 100.0
BOUND_SLACK = 1.02
KV_UNROLL = 4
KEY_TILE = 512
FFN_TILE = 512
FFN_ROWS = 1024
ROW_TILE = 512
MLA_QUERIES = 1024
HALO = 16
NORM_ROWS = 256
VMEM_LIMIT = 56 * 1024 * 1024

C_QA, C_SGU, C_SGV, C_DAQ, C_DAK, C_DAV, C_KVA, C_KR, IN_PAD = (
    0, 512, 1024, 1536, 2304, 3072, 3840, 4096, 4224)

BF16 = jnp.bfloat16
F32 = jnp.float32


def _cp(*sem):
    return pltpu.CompilerParams(dimension_semantics=sem, vmem_limit_bytes=VMEM_LIMIT)


def _dot(a, b):
    return jnp.dot(a, b, preferred_element_type=F32)


def _dot_nt(a, b):
    return lax.dot_general(a, b, (((1,), (1,)), ((), ())), preferred_element_type=F32)


def _rms_scale(x, n):
    return lax.rsqrt(jnp.sum(x * x, axis=-1, keepdims=True) * (1.0 / n) + EPS)


def _same_group_matrix(group):
    r = lax.broadcasted_iota(jnp.int32, (LANES, LANES), 0)
    c = lax.broadcasted_iota(jnp.int32, (LANES, LANES), 1)
    shift = group.bit_length() - 1
    same = lax.shift_right_logical(r, shift) == lax.shift_right_logical(c, shift)
    return jnp.where(same, 1.0, 0.0).astype(BF16)


def _relayout_kernel(src_ref, *dst_refs, plans):
    x = src_ref[0]
    for dst_ref, moves in zip(dst_refs, plans):
        covered = 0
        for d0, s0, w in sorted(moves):
            if d0 > covered:
                dst_ref[0, :, covered:d0] = jnp.zeros((x.shape[0], d0 - covered), BF16)
            dst_ref[0, :, d0:d0 + w] = x[:, s0:s0 + w].astype(BF16)
            covered = d0 + w
        width = dst_ref.shape[-1]
        if covered < width:
            dst_ref[0, :, covered:] = jnp.zeros((x.shape[0], width - covered), BF16)


def _relayout(w, plans, widths, rows):
    L, R, C = w.shape
    return pl.pallas_call(
        functools.partial(_relayout_kernel, plans=plans),
        grid=(L, R // rows),
        in_specs=[pl.BlockSpec((1, rows, C), lambda l, i: (l, i, 0))],
        out_specs=[pl.BlockSpec((1, rows, n), lambda l, i: (l, i, 0)) for n in widths],
        out_shape=[jax.ShapeDtypeStruct((L, R, n), BF16) for n in widths],
        compiler_params=_cp("parallel", "parallel"),
        name="weight_relayout",
    )(w)


def _row_relayout_kernel(src_ref, dst_ref, *, moves):
    covered = 0
    cols = dst_ref.shape[-1]
    for d0, s0, n in sorted(moves):
        if d0 > covered:
            dst_ref[0, covered:d0, :] = jnp.zeros((d0 - covered, cols), BF16)
        dst_ref[0, d0:d0 + n, :] = src_ref[0, s0:s0 + n, :].astype(BF16)
        covered = d0 + n
    if covered < dst_ref.shape[1]:
        dst_ref[0, covered:, :] = jnp.zeros((dst_ref.shape[1] - covered, cols), BF16)


def _row_relayout(wt, moves, rows_out, cols):
    L, R, C = wt.shape
    return pl.pallas_call(
        functools.partial(_row_relayout_kernel, moves=moves),
        grid=(L, C // cols),
        in_specs=[pl.BlockSpec((1, R, cols), lambda l, i: (l, 0, i))],
        out_specs=pl.BlockSpec((1, rows_out, cols), lambda l, i: (l, 0, i)),
        out_shape=jax.ShapeDtypeStruct((L, rows_out, C), BF16),
        compiler_params=_cp("parallel", "parallel"),
        name="weight_row_relayout",
    )(wt)


def _ada_kernel(ct_ref, w_ref, b_ref, o_ref):
    c = ct_ref[...]
    cond = c / (1.0 + jnp.exp(-c))
    w = w_ref[0]
    for b in range(c.shape[1]):
        o_ref[0, b:b + 1, :] = jnp.sum(w * cond[:, b:b + 1], axis=0, keepdims=True) + b_ref[0]


def _ada_mod(c, w_ada, b_ada):
    L, D, N = w_ada.shape
    B = c.shape[0]
    tn = 1024
    return pl.pallas_call(
        _ada_kernel,
        grid=(L, N // tn),
        in_specs=[pl.BlockSpec((D, B), lambda l, j: (0, 0)),
                  pl.BlockSpec((1, D, tn), lambda l, j: (l, 0, j)),
                  pl.BlockSpec((1, 1, tn), lambda l, j: (l, 0, j))],
        out_specs=pl.BlockSpec((1, B, tn), lambda l, j: (l, 0, j)),
        out_shape=jax.ShapeDtypeStruct((L, B, N), F32),
        compiler_params=_cp("parallel", "parallel"),
        name="ada_mod",
    )(c.T, w_ada, b_ada.reshape(L, 1, N))


def _rope_table_kernel(pos_ref, inv_ref, c_ref, s1_ref, s2_ref):
    ang = pos_ref[...] * inv_ref[...]
    lane = lax.broadcasted_iota(jnp.int32, ang.shape, 1)
    half = MLA_ROPE // 2
    cos = jnp.cos(ang)
    sin = jnp.sin(ang)
    c_ref[...] = jnp.where(lane < MLA_ROPE, cos, 0.0)
    s1_ref[...] = jnp.where(lane < half, -sin, 0.0)
    s2_ref[...] = jnp.where((lane >= half) & (lane < MLA_ROPE), sin, 0.0)


def _rope_tables(pos_col):
    T = pos_col.shape[0]
    half = MLA_ROPE // 2
    inv = ROPE_THETA ** (-jnp.arange(half, dtype=F32) / half)
    inv_row = jnp.concatenate([inv, inv, jnp.zeros((LANES - MLA_ROPE,), F32)]).reshape(1, LANES)
    tm = min(T, 2048)
    sds = jax.ShapeDtypeStruct((T, LANES), F32)
    spec = pl.BlockSpec((tm, LANES), lambda i: (i, 0))
    return pl.pallas_call(
        _rope_table_kernel,
        grid=(T // tm,),
        in_specs=[pl.BlockSpec((tm, 1), lambda i: (i, 0)),
                  pl.BlockSpec((1, LANES), lambda i: (0, 0))],
        out_specs=[spec, spec, spec],
        out_shape=[sds, sds, sds],
        compiler_params=_cp("parallel"),
        name="rope_tables",
    )(pos_col, inv_row)


def _norm_matmul_kernel(x_ref, sh_ref, sc_ref, w_ref, o_ref, h_scr):
    @pl.when(pl.program_id(1) == 0)
    def _():
        for r in range(0, x_ref.shape[0], NORM_ROWS):
            x = x_ref[r:r + NORM_ROWS, :]
            h = x * _rms_scale(x, x.shape[-1]) * (1.0 + sc_ref[0]) + sh_ref[0]
            h_scr[r:r + NORM_ROWS, :] = h.astype(BF16)

    o_ref[...] = _dot_nt(h_scr[...], w_ref[...]).astype(o_ref.dtype)


def _norm_matmul(x2, mod_l, which, wt, l, S, tm, tn):
    T, D = x2.shape
    N = wt.shape[1]
    tpb = S // tm
    w_mode = dict(pipeline_mode=pl.Buffered(1)) if tn == N else {}
    return pl.pallas_call(
        _norm_matmul_kernel,
        grid=(T // tm, N // tn),
        in_specs=[pl.BlockSpec((tm, D), lambda i, j: (i, 0)),
                  pl.BlockSpec((1, 1, D), lambda i, j: (i // tpb, 0, which)),
                  pl.BlockSpec((1, 1, D), lambda i, j: (i // tpb, 0, which + 1)),
                  pl.BlockSpec((None, tn, D), lambda i, j: (l, j, 0), **w_mode)],
        out_specs=pl.BlockSpec((tm, tn), lambda i, j: (i, j)),
        out_shape=jax.ShapeDtypeStruct((T, N), BF16),
        scratch_shapes=[pltpu.VMEM((tm, D), BF16)],
        compiler_params=_cp("parallel", "arbitrary"),
        name="norm_matmul",
    )(x2, mod_l, mod_l, wt)


def _da_prep_kernel(zq_ref, zk_ref, zv_ref, d_ref, gq_ref, gk_ref, qc_ref, kc_ref, q_ref, k_ref, vt_ref):
    lane = lax.broadcasted_iota(jnp.int32, (1, LANES), 1)
    lo = lane < DA_QK
    same_map = _same_group_matrix(DA_QK)

    def qk_norm(z, g):
        ss = _dot((z * z).astype(BF16), same_map)
        return z * lax.rsqrt(ss * (1.0 / DA_QK) + EPS) * g

    d = d_ref[...]
    d_hi = jnp.floor(d * (1.0 / LANES)) * LANES
    d_lo = d - d_hi
    k_pos = jnp.where(lane < 3, d_hi, jnp.where(lane < 6, d_lo, 0.0))
    q_pos = jnp.where((lane >= 6) & (lane < 9), -d_hi, jnp.where((lane >= 9) & (lane < 12), -d_lo, 0.0))
    for h in range(DA_HEADS):
        sl = slice(h * LANES, (h + 1) * LANES)
        qn = qk_norm(zq_ref[:, sl].astype(F32), gq_ref[...])
        kn = qk_norm(zk_ref[:, sl].astype(F32), gk_ref[...])
        qa = q_pos + qc_ref[h]
        qa_t = qa.T.astype(BF16)
        q_ref[0, h, 0, :LANES, :] = jnp.where(lo, qn, 0.0).T.astype(BF16)
        q_ref[0, h, 0, LANES:, :] = qa_t
        q_ref[0, h, 1, :LANES, :] = jnp.where(lo, 0.0, qn).T.astype(BF16)
        q_ref[0, h, 1, LANES:, :] = qa_t
        k_ref[0, h, :, :LANES] = kn.astype(BF16)
        k_ref[0, h, :, LANES:] = (k_pos + kc_ref[h]).astype(BF16)
        vt_ref[0, h] = zv_ref[:, sl].astype(F32).T.astype(BF16)


def _da_prep(z, d_col, gq, gk, qc, kc, B, S, tm):
    T = z.shape[0]
    tpb = S // tm
    H = DA_HEADS
    return pl.pallas_call(
        _da_prep_kernel,
        grid=(T // tm,),
        in_specs=[pl.BlockSpec((tm, DA_WIDTH), lambda i: (i, C_DAQ // DA_WIDTH)),
                  pl.BlockSpec((tm, DA_WIDTH), lambda i: (i, C_DAK // DA_WIDTH)),
                  pl.BlockSpec((tm, DA_WIDTH), lambda i: (i, C_DAV // DA_WIDTH)),
                  pl.BlockSpec((tm, 1), lambda i: (i, 0)),
                  pl.BlockSpec((1, LANES), lambda i: (0, 0)),
                  pl.BlockSpec((1, LANES), lambda i: (0, 0)),
                  pl.BlockSpec((H, 1, LANES), lambda i: (0, 0, 0)),
                  pl.BlockSpec((H, 1, LANES), lambda i: (0, 0, 0))],
        out_specs=[pl.BlockSpec((1, H, 2, QK_PAD, tm), lambda i: (i // tpb, 0, 0, 0, i % tpb)),
                   pl.BlockSpec((1, H, tm, QK_PAD), lambda i: (i // tpb, 0, i % tpb, 0)),
                   pl.BlockSpec((1, H, DA_V, tm), lambda i: (i // tpb, 0, 0, i % tpb))],
        out_shape=[jax.ShapeDtypeStruct((B, H, 2, QK_PAD, S), BF16),
                   jax.ShapeDtypeStruct((B, H, S, QK_PAD), BF16),
                   jax.ShapeDtypeStruct((B, H, DA_V, S), BF16)],
        compiler_params=_cp("parallel"),
        name="da_prep",
    )(z, z, z, d_col, gq, gk, qc, kc)


def _mla_prep_kernel(zqa_ref, zkva_ref, zkr_ref, gqa_ref, gkva_ref, wqn_ref, wqr_ref, wkn_ref, wv_ref,
                     gqn_ref, gqr_ref, gkn_ref, gkr_ref, qb_ref, kb_ref, c_ref, s1_ref, s2_ref,
                     q_ref, k_ref, v_ref):
    half = MLA_ROPE // 2
    width = MLA_NOPE + MLA_ROPE
    cos, sin1, sin2 = c_ref[...], s1_ref[...], s2_ref[...]

    def rope(t):
        return (t * cos + pltpu.roll(t, LANES - half, 1) * sin1 + pltpu.roll(t, half, 1) * sin2)

    zqa = zqa_ref[...].astype(F32)
    qa = (zqa * _rms_scale(zqa, MLA_Q_RANK) * gqa_ref[...]).astype(BF16)
    zkva = zkva_ref[...].astype(F32)
    kva = (zkva * _rms_scale(zkva, MLA_KV_RANK) * gkva_ref[...]).astype(BF16)
    qn_all = _dot(qa, wqn_ref[...])
    qr_all = _dot(qa, wqr_ref[...])
    kn_all = _dot(kva, wkn_ref[...])
    v_all = _dot(kva, wv_ref[...])
    kr = zkr_ref[...].astype(F32)
    kr_sq = kr * kr
    kr_rot = rope(kr * gkr_ref[...])
    ones = _same_group_matrix(LANES)
    for h in range(MLA_HEADS):
        sl = slice(h * LANES, (h + 1) * LANES)
        qn, qr, kn = qn_all[:, sl], qr_all[:, sl], kn_all[:, sl]
        q_inv = lax.rsqrt(_dot((qn * qn + qr * qr).astype(BF16), ones) * (1.0 / width) + EPS)
        k_inv = lax.rsqrt(_dot((kn * kn + kr_sq).astype(BF16), ones) * (1.0 / width) + EPS)
        q_ref[0, h, :LANES, :] = (qn * q_inv * gqn_ref[...]).T.astype(BF16)
        q_ref[0, h, LANES:, :] = (rope(qr * q_inv * gqr_ref[...]) + qb_ref[...]).T.astype(BF16)
        k_ref[0, h, :, :LANES] = (kn * k_inv * gkn_ref[...]).astype(BF16)
        k_ref[0, h, :, LANES:] = (kr_rot * k_inv + kb_ref[...]).astype(BF16)
        v_ref[0, h] = v_all[:, sl].T.astype(BF16)


def _mla_prep(z, gqa, gkva, wqn, wqr, wkn, wv, l, gqn, gqr, gkn, gkr, qb, kb, tabs, B, S, tm):
    T = z.shape[0]
    tpb = S // tm
    H = MLA_HEADS
    full = lambda a: pl.BlockSpec(a.shape, lambda i: (0,) * a.ndim)
    layer = lambda a: pl.BlockSpec((None,) + a.shape[1:], lambda i: (l, 0, 0))
    row = pl.BlockSpec((tm, LANES), lambda i: (i, 0))
    return pl.pallas_call(
        _mla_prep_kernel,
        grid=(T // tm,),
        in_specs=[pl.BlockSpec((tm, MLA_Q_RANK), lambda i: (i, C_QA // MLA_Q_RANK)),
                  pl.BlockSpec((tm, MLA_KV_RANK), lambda i: (i, C_KVA // MLA_KV_RANK)),
                  pl.BlockSpec((tm, LANES), lambda i: (i, C_KR // LANES)),
                  full(gqa), full(gkva), layer(wqn), layer(wqr), layer(wkn), layer(wv),
                  full(gqn), full(gqr), full(gkn), full(gkr), full(qb), full(kb), row, row, row],
        out_specs=[pl.BlockSpec((1, H, QK_PAD, tm), lambda i: (i // tpb, 0, 0, i % tpb)),
                   pl.BlockSpec((1, H, tm, QK_PAD), lambda i: (i // tpb, 0, i % tpb, 0)),
                   pl.BlockSpec((1, H, MLA_V, tm), lambda i: (i // tpb, 0, 0, i % tpb))],
        out_shape=[jax.ShapeDtypeStruct((B, H, QK_PAD, S), BF16),
                   jax.ShapeDtypeStruct((B, H, S, QK_PAD), BF16),
                   jax.ShapeDtypeStruct((B, H, MLA_V, S), BF16)],
        compiler_params=_cp("parallel"),
        name="mla_prep",
    )(z, z, z, gqa, gkva, wqn, wqr, wkn, wv, gqn, gqr, gkn, gkr, qb, kb, *tabs)


def _gelu_tanh(x):
    return 0.5 * x * (1.0 + jnp.tanh(0.7978845608028654 * (x + 0.044715 * (x * x * x))))


def _sg_kernel(zu_ref, zv_ref, g_ref, w_ref, b_ref, o_ref):
    tm = zu_ref.shape[0]
    r = lax.broadcasted_iota(jnp.int32, (SG_LEN, SG_LEN), 0)
    c = lax.broadcasted_iota(jnp.int32, (SG_LEN, SG_LEN), 1)
    causal = c <= r
    for g in range(SG_GROUPS):
        sl = slice(g * SG_CH, (g + 1) * SG_CH)
        wt = jnp.where(causal, w_ref[g], 0.0).astype(BF16)
        v = _gelu_tanh(zv_ref[:, sl].astype(F32))
        v = (v * _rms_scale(v, SG_CH) * g_ref[:, sl]).astype(BF16)
        u = _gelu_tanh(zu_ref[:, sl].astype(F32))
        for n in range(tm // SG_LEN):
            rows = slice(n * SG_LEN, (n + 1) * SG_LEN)
            s = _dot(wt, v[rows, :]) + b_ref[:, sl]
            o_ref[rows, sl] = (u[rows, :] * s).astype(BF16)


def _sg(z, gain, w, bias, tm):
    T = z.shape[0]
    return pl.pallas_call(
        _sg_kernel,
        grid=(T // tm,),
        in_specs=[pl.BlockSpec((tm, SG_WIDTH), lambda i: (i, C_SGU // SG_WIDTH)),
                  pl.BlockSpec((tm, SG_WIDTH), lambda i: (i, C_SGV // SG_WIDTH)),
                  pl.BlockSpec((1, SG_WIDTH), lambda i: (0, 0)),
                  pl.BlockSpec((SG_GROUPS, SG_LEN, SG_LEN), lambda i: (0, 0, 0)),
                  pl.BlockSpec((SG_LEN, SG_WIDTH), lambda i: (0, 0))],
        out_specs=pl.BlockSpec((tm, SG_WIDTH), lambda i: (i, 0)),
        out_shape=jax.ShapeDtypeStruct((T, SG_WIDTH), BF16),
        compiler_params=_cp("parallel"),
        name="spatial_gating",
    )(z, z, gain, w, bias)


def _flash_t(qt, k_ref, vt_ref, n_off, diag_fixes, online):
    n = qt.shape[1]

    def step(j, state, fix=None):
        m, l, acc = state
        start = pl.multiple_of(j * KEY_TILE, KEY_TILE)
        s = _dot(k_ref[0, 0, pl.ds(start, KEY_TILE), :], qt)
        if fix is not None:
            s = fix(s)
        if online:
            m_new = jnp.maximum(m, jnp.max(s, axis=0, keepdims=True))
            alpha = jnp.exp2(m - m_new)
            p = jnp.exp2(s - m_new)
            l = alpha * l
            acc = alpha * acc
        else:
            m_new = m
            p = jnp.exp2(s)
        l = l + jnp.sum(p, axis=0, keepdims=True)
        acc = acc + _dot(vt_ref[0, 0, :, pl.ds(start, KEY_TILE)], p.astype(BF16))
        return m_new, l, acc

    def run(j0, count, state):
        for u in range(count):
            state = step(j0 + u, state)
        return state

    state = (jnp.full((1, n), NEG_BIG, F32), jnp.zeros((1, n), F32), jnp.zeros((DA_V, n), F32))
    n_grp = n_off // KV_UNROLL
    state = lax.fori_loop(0, n_grp, lambda g, st: run(g * KV_UNROLL, KV_UNROLL, st), state)
    state = lax.fori_loop(n_grp * KV_UNROLL, n_off, lambda j, st: run(j, 1, st), state)
    for t, fix in enumerate(diag_fixes):
        state = step(n_off + t, state, fix)
    _, l, acc = state
    return acc / l


def _chunk_visible(n, tq, key_off):
    r = lax.broadcasted_iota(jnp.int32, (KEY_TILE, n), 0) + key_off
    c = lax.broadcasted_iota(jnp.int32, (KEY_TILE, n), 1)
    if n > tq:
        c = jnp.where(c >= tq, c - tq, c)
    shift = CHUNK.bit_length() - 1
    return lax.shift_right_logical(r, shift) <= lax.shift_right_logical(c, shift)


def _da_attn_kernel(q_ref, k_ref, vt_ref, dk_ref, dq_ref, sl2_ref, lam_ref, hg_ref, o_ref, *,
                    tq, lam_init, online):
    qi = pl.program_id(2)
    qt = jnp.concatenate([q_ref[0, 0, 0], q_ref[0, 0, 1]], axis=1)

    def fix(s):
        dq = dq_ref[0]
        dq2 = jnp.concatenate([dq, dq], axis=1)
        corr = sl2_ref[0, :, :1] * jnp.maximum(dk_ref[...] - dq2, 0.0)
        return jnp.where(_chunk_visible(2 * tq, tq, 0), s - corr, NEG_BIG)

    ot = _flash_t(qt, k_ref, vt_ref, qi, [fix], online)
    lv = lam_ref[...]
    lam = (jnp.exp(jnp.sum(lv[0:1] * lv[1:2], axis=-1, keepdims=True))
           - jnp.exp(jnp.sum(lv[2:3] * lv[3:4], axis=-1, keepdims=True)) + lam_init)
    od = ot[:, :tq] - lam * ot[:, tq:]
    inv = lax.rsqrt(jnp.sum(od * od, axis=0, keepdims=True) * (1.0 / DA_V) + EPS)
    od = od * inv * (1.0 - lam_init)
    o_ref[...] = (od.T * hg_ref[0]).astype(o_ref.dtype)


def _da_attention(q, k, vt, d_col, d_row, sl2, lam_vecs, hg, B, S, tq, lam_init, online):
    H = DA_HEADS
    nq = S // tq
    assert tq == KEY_TILE
    return pl.pallas_call(
        functools.partial(_da_attn_kernel, tq=tq, lam_init=lam_init, online=online),
        grid=(B, H, nq),
        in_specs=[pl.BlockSpec((1, 1, 2, QK_PAD, tq), lambda b, h, i: (b, h, 0, 0, i)),
                  pl.BlockSpec((1, 1, S, QK_PAD), lambda b, h, i: (b, h, 0, 0)),
                  pl.BlockSpec((1, 1, DA_V, S), lambda b, h, i: (b, h, 0, 0)),
                  pl.BlockSpec((tq, 1), lambda b, h, i: (b * nq + i, 0)),
                  pl.BlockSpec((1, 1, tq), lambda b, h, i: (b, 0, i)),
                  pl.BlockSpec((1, 1, LANES), lambda b, h, i: (h, 0, 0)),
                  pl.BlockSpec((4, DA_QK), lambda b, h, i: (0, 0)),
                  pl.BlockSpec((1, 1, DA_V), lambda b, h, i: (h, 0, 0))],
        out_specs=pl.BlockSpec((tq, DA_V), lambda b, h, i: (b * nq + i, h)),
        out_shape=jax.ShapeDtypeStruct((B * S, DA_WIDTH), BF16),
        compiler_params=_cp("parallel", "parallel", "parallel"),
        name="da_attention",
    )(q, k, vt, d_col, d_row, sl2, lam_vecs, hg)


def _mla_attn_kernel(q_ref, k_ref, vt_ref, o_ref, *, tq, online):
    qi = pl.program_id(2)
    tiles = tq // KEY_TILE
    fixes = [lambda s, t=t: jnp.where(_chunk_visible(tq, tq, t * KEY_TILE), s, NEG_BIG) for t in range(tiles)]
    ot = _flash_t(q_ref[0, 0], k_ref, vt_ref, qi * tiles, fixes, online)
    o_ref[...] = ot.T.astype(o_ref.dtype)


def _mla_attention(q, k, vt, B, S, tq, online):
    H = MLA_HEADS
    nq = S // tq
    return pl.pallas_call(
        functools.partial(_mla_attn_kernel, tq=tq, online=online),
        grid=(B, H, nq),
        in_specs=[pl.BlockSpec((1, 1, QK_PAD, tq), lambda b, h, i: (b, h, 0, i)),
                  pl.BlockSpec((1, 1, S, QK_PAD), lambda b, h, i: (b, h, 0, 0)),
                  pl.BlockSpec((1, 1, MLA_V, S), lambda b, h, i: (b, h, 0, 0))],
        out_specs=pl.BlockSpec((tq, MLA_V), lambda b, h, i: (b * nq + i, h)),
        out_shape=jax.ShapeDtypeStruct((B * S, MLA_WIDTH), BF16),
        compiler_params=_cp("parallel", "parallel", "parallel"),
        name="mla_attention",
    )(q, k, vt)


def _out_proj_kernel(a_ref, b_ref, c_ref, w_ref, x_ref, g_ref, o_ref):
    ka, kb = a_ref.shape[1], a_ref.shape[1] + b_ref.shape[1]
    mix = (_dot(a_ref[...], w_ref[:ka, :].astype(BF16)) + _dot(b_ref[...], w_ref[ka:kb, :].astype(BF16))
           + _dot(c_ref[...], w_ref[kb:, :].astype(BF16)))
    o_ref[...] = x_ref[...] + g_ref[0] * mix


def _out_proj(a, b, c, w, l, x2, mod_l, S, tm, tn):
    T, D = x2.shape
    tpb = S // tm
    nb = D // tn
    w_mode = dict(pipeline_mode=pl.Buffered(1)) if nb == 1 else {}
    return pl.pallas_call(
        _out_proj_kernel,
        grid=(T // tm, nb),
        in_specs=[pl.BlockSpec((tm, DA_WIDTH), lambda i, j: (i, 0)),
                  pl.BlockSpec((tm, MLA_WIDTH), lambda i, j: (i, 0)),
                  pl.BlockSpec((tm, SG_WIDTH), lambda i, j: (i, 0)),
                  pl.BlockSpec((None, w.shape[1], tn), lambda i, j: (l, 0, j), **w_mode),
                  pl.BlockSpec((tm, tn), lambda i, j: (i, j)),
                  pl.BlockSpec((1, 1, tn), lambda i, j: (i // tpb, 0, 2 * nb + j))],
        out_specs=pl.BlockSpec((tm, tn), lambda i, j: (i, j)),
        out_shape=jax.ShapeDtypeStruct((T, D), F32),
        compiler_params=_cp("parallel", "parallel"),
        name="out_proj",
    )(a, b, c, w, x2, mod_l)


def _ffn_kernel(x_ref, xh_ref, sh_ref, sc_ref, g_ref, wg_ref, wv_ref, cwg_ref, cwv_ref, cbg_ref, cbv_ref,
                wd_ref, o_ref, h_scr, *, tpb):
    i = pl.program_id(0)
    j = pl.program_id(1)

    @pl.when(j == 0)
    def _():
        def modnorm(x):
            return x * _rms_scale(x, x.shape[-1]) * (1.0 + sc_ref[0]) + sh_ref[0]

        for r in range(0, x_ref.shape[0], NORM_ROWS):
            h_scr[HALO + r:HALO + r + NORM_ROWS, :] = modnorm(x_ref[r:r + NORM_ROWS, :]).astype(BF16)
        keep = jnp.where(i % tpb == 0, 0.0, 1.0)
        h_scr[:HALO, :] = (modnorm(xh_ref[...]) * keep).astype(BF16)
        o_ref[...] = jnp.zeros_like(o_ref)

    h = h_scr[...]

    def conv(a, cw_ref, cb_ref):
        y = (cb_ref[...] + a * cw_ref[2:3, :] + pltpu.roll(a, 1, 0) * cw_ref[1:2, :]
             + pltpu.roll(a, 2, 0) * cw_ref[0:1, :])
        return y[HALO:, :]

    yg = conv(_dot(h, wg_ref[...]), cwg_ref, cbg_ref)
    yv = conv(_dot(h, wv_ref[...]), cwv_ref, cbv_ref)
    act = (yg / (1.0 + jnp.exp(-yg)) * yv).astype(BF16)
    o_ref[...] += _dot(act, wd_ref[...].astype(BF16))

    @pl.when(j == pl.num_programs(1) - 1)
    def _():
        o_ref[...] = x_ref[...] + g_ref[0] * o_ref[...]


def _ffn(x2, mod_l, w_up, conv_w, conv_b, w_down, l, S, tm, tf):
    T, D = x2.shape
    F = w_down.shape[1]
    nf = F // tf
    tpb = S // tm
    hb = tm // HALO
    return pl.pallas_call(
        functools.partial(_ffn_kernel, tpb=tpb),
        grid=(T // tm, nf),
        in_specs=[pl.BlockSpec((tm, D), lambda i, j: (i, 0)),
                  pl.BlockSpec((HALO, D), lambda i, j: (jnp.maximum(i * hb - 1, 0), 0)),
                  pl.BlockSpec((1, 1, D), lambda i, j: (i // tpb, 0, 3)),
                  pl.BlockSpec((1, 1, D), lambda i, j: (i // tpb, 0, 4)),
                  pl.BlockSpec((1, 1, D), lambda i, j: (i // tpb, 0, 5)),
                  pl.BlockSpec((None, D, tf), lambda i, j: (l, 0, j)),
                  pl.BlockSpec((None, D, tf), lambda i, j: (l, 0, nf + j)),
                  pl.BlockSpec((None, CONV_W, tf), lambda i, j: (l, 0, j)),
                  pl.BlockSpec((None, CONV_W, tf), lambda i, j: (l, 0, nf + j)),
                  pl.BlockSpec((None, 1, tf), lambda i, j: (l, 0, j)),
                  pl.BlockSpec((None, 1, tf), lambda i, j: (l, 0, nf + j)),
                  pl.BlockSpec((None, tf, D), lambda i, j: (l, j, 0))],
        out_specs=pl.BlockSpec((tm, D), lambda i, j: (i, 0), pipeline_mode=pl.Buffered(1)),
        out_shape=jax.ShapeDtypeStruct((T, D), F32),
        scratch_shapes=[pltpu.VMEM((HALO + tm, D), BF16)],
        compiler_params=_cp("parallel", "arbitrary"),
        name="conv_ffn",
    )(x2, x2, mod_l, mod_l, mod_l, w_up, w_up, conv_w, conv_w, conv_b, conv_b, w_down)


def _split3(v):
    a = v.astype(BF16).astype(F32)
    b = (v - a).astype(BF16).astype(F32)
    c = (v - a - b).astype(BF16).astype(F32)
    return a, b, c


def kernel(x, c, positions, w_ada, b_ada, w_in, da_q_gain, da_k_gain, da_lq1, da_lk1, da_lq2, da_lk2, da_head_gain, mla_q_a_gain, mla_w_uq, mla_kv_a_gain, mla_w_ukv, mla_q_gain, mla_k_gain, sg_v_gain, sg_w, sg_b, w_out, ffn_w_up, ffn_conv_w, ffn_conv_b, ffn_w_down):
    B, S, D = x.shape
    L = w_ada.shape[0]
    T = B * S
    assert S % ROW_TILE == 0 and S % KEY_TILE == 0
    tm_big = min(S, FFN_ROWS)
    tm = ROW_TILE
    tq_mla = min(S, MLA_QUERIES)

    mod = _ada_mod(c, w_ada, b_ada)

    d = (positions - positions[:, :1]).astype(F32)
    d_col = d.reshape(T, 1)
    d_row = d.reshape(B, 1, S)
    tabs = _rope_tables(positions.astype(F32).reshape(T, 1))

    H = DA_HEADS
    sl = jnp.asarray(ALIBI_SLOPES, F32) * LOG2E
    slope6 = jnp.tile(jnp.stack(_split3(sl), axis=-1), (1, 2))
    zeros = lambda n: jnp.zeros((H, n), F32)
    kc = jnp.concatenate([zeros(6), slope6, jnp.ones((H, 3), F32), zeros(LANES - 15)], axis=-1)
    kc = kc.reshape(H, 1, LANES)
    sl2 = jnp.broadcast_to((2.0 * sl)[:, None, None], (H, 1, LANES))

    da_scale = DA_QK ** -0.5 * LOG2E
    mla_width = MLA_NOPE + MLA_ROPE
    mla_scale = mla_width ** -0.5 * LOG2E
    pad_r = lambda g: jnp.pad(g, (0, LANES - MLA_ROPE)).reshape(1, LANES)
    shift_lanes = lambda v: jnp.pad(v, (MLA_ROPE, LANES - MLA_ROPE - 3)).reshape(1, LANES)
    kb_mla = shift_lanes(jnp.ones((3,), F32))
    gmax = lambda g: jnp.max(jnp.abs(g))

    in_moves = [(C_QA, 2304, 512), (C_SGU, 3136, 512), (C_SGV, 3648, 512), (C_DAQ, 0, 768),
                (C_DAK, 768, 768), (C_DAV, 1536, 768), (C_KVA, 2816, 256), (C_KR, 3072, MLA_ROPE)]
    w_in_p = _row_relayout(jnp.swapaxes(w_in, 1, 2), in_moves, IN_PAD, 512)
    heads = range(MLA_HEADS)
    wqn_all, wqr_all = _relayout(
        mla_w_uq,
        [[(h * LANES, h * mla_width, MLA_NOPE) for h in heads],
         [(h * LANES, h * mla_width + MLA_NOPE, MLA_ROPE) for h in heads]],
        [MLA_WIDTH, MLA_HEADS * LANES], 256)
    kv_width = MLA_NOPE + MLA_V
    wkn_all, wv_all = _relayout(
        mla_w_ukv,
        [[(h * LANES, h * kv_width, MLA_NOPE) for h in heads],
         [(h * LANES, h * kv_width + MLA_NOPE, MLA_V) for h in heads]],
        [MLA_WIDTH, MLA_WIDTH], 256)
    w_up_b = ffn_w_up.astype(BF16)
    conv_b3 = ffn_conv_b.reshape(L, 1, -1)

    x2 = x.reshape(T, D)
    for l in range(L):
        mod_l = mod[l].reshape(B, 1, 6 * D)
        z = _norm_matmul(x2, mod_l, 0, w_in_p, l, S, tm, IN_PAD)

        lam_init = 0.8 - 0.6 * math.exp(-0.3 * l)
        gq = (jnp.tile(da_q_gain[l], 2) * da_scale).reshape(1, LANES)
        gk = jnp.tile(da_k_gain[l], 2).reshape(1, LANES)
        bound = BOUND_SLACK * DA_QK * da_scale * gmax(da_q_gain[l]) * gmax(da_k_gain[l])
        nb = jnp.broadcast_to(-jnp.stack(_split3(bound)), (H, 3))
        qc = jnp.concatenate([slope6, zeros(6), nb, zeros(LANES - 15)], axis=-1).reshape(H, 1, LANES)
        q_da, k_da, vt_da = _da_prep(z, d_col, gq, gk, qc, kc, B, S, tm)
        lam_vecs = jnp.stack([da_lq1[l], da_lk1[l], da_lq2[l], da_lk2[l]])
        da_args = (q_da, k_da, vt_da, d_col, d_row, sl2, lam_vecs, da_head_gain[l].reshape(H, 1, DA_V))
        out_a = lax.cond(
            2.0 * bound <= MAX_SHIFT,
            lambda a: _da_attention(*a, B, S, KEY_TILE, lam_init, online=False),
            lambda a: _da_attention(*a, B, S, KEY_TILE, lam_init, online=True), da_args)

        qg, kg = mla_q_gain[l], mla_k_gain[l]
        bound_m = BOUND_SLACK * mla_width * mla_scale * gmax(qg) * gmax(kg)
        q_m, k_m, v_m = _mla_prep(
            z, mla_q_a_gain[l].reshape(1, -1), mla_kv_a_gain[l].reshape(1, -1),
            wqn_all, wqr_all, wkn_all, wv_all, l,
            (qg[:MLA_NOPE] * mla_scale).reshape(1, LANES), pad_r(qg[MLA_NOPE:] * mla_scale),
            kg[:MLA_NOPE].reshape(1, LANES), pad_r(kg[MLA_NOPE:]),
            shift_lanes(-jnp.stack(_split3(bound_m))), kb_mla, tabs, B, S, tm)
        out_b = lax.cond(
            2.0 * bound_m <= MAX_SHIFT,
            lambda a: _mla_attention(*a, B, S, tq_mla, online=False),
            lambda a: _mla_attention(*a, B, S, tq_mla, online=True), (q_m, k_m, v_m))

        sg_bias = jnp.repeat(sg_b[l].T, SG_CH, axis=1)
        out_c = _sg(z, sg_v_gain[l].reshape(1, SG_WIDTH), sg_w[l], sg_bias, tm)

        x2 = _out_proj(out_a, out_b, out_c, w_out, l, x2, mod_l, S, tm, D)
        x2 = _ffn(x2, mod_l, w_up_b, ffn_conv_w, conv_b3, ffn_w_down, l, S, tm_big, FFN_TILE)
    return x2.reshape(B, S, D)
```

```python
import functools
import math

import jax
import jax.numpy as jnp
from jax import lax
from jax.experimental import pallas as pl
from jax.experimental.pallas import tpu as pltpu

CHUNK = 64
EPS = 1e-6
DA_HEADS = 6
DA_QK = 64
DA_V = 128
MLA_HEADS = 6
MLA_Q_RANK = 512
MLA_KV_RANK = 256
MLA_NOPE = 128
MLA_ROPE = 64
MLA_V = 128
ROPE_THETA = 10000.0
SG_GROUPS = 4
SG_CH = 128
SG_LEN = 128
CONV_W = 3
DA_WIDTH = DA_HEADS * DA_V
MLA_WIDTH = MLA_HEADS * MLA_V
SG_WIDTH = SG_GROUPS * SG_CH
ALIBI_SLOPES = tuple(2.0 ** (-8.0 * (h + 1) / DA_HEADS) for h in range(DA_HEADS))

LOG2E = 1.4426950408889634
LANES = 128
QK_PAD = 256
NEG_BIG = -1e30
MAX_SHIFT = 100.0
BOUND_SLACK = 1.02
KV_UNROLL = 4
KEY_TILE = 512
FFN_TILE = 512
FFN_ROWS = 1024
ROW_TILE = 512
MLA_QUERIES = 1024
HALO = 16
NORM_ROWS = 256
VMEM_LIMIT = 56 * 1024 * 1024

C_QA, C_SGU, C_SGV, C_DAQ, C_DAK, C_DAV, C_KVA, C_KR, IN_PAD = (
    0, 512, 1024, 1536, 2304, 3072, 3840, 4096, 4224)

BF16 = jnp.bfloat16
F32 = jnp.float32


def _cp(*sem):
    return pltpu.CompilerParams(dimension_semantics=sem, vmem_limit_bytes=VMEM_LIMIT)


def _dot(a, b):
    return jnp.dot(a, b, preferred_element_type=F32)


def _dot_nt(a, b):
    return lax.dot_general(a, b, (((1,), (1,)), ((), ())), preferred_element_type=F32)


def _rms_scale(x, n):
    return lax.rsqrt(jnp.sum(x * x, axis=-1, keepdims=True) * (1.0 / n) + EPS)


def _same_group_matrix(group):
    r = lax.broadcasted_iota(jnp.int32, (LANES, LANES), 0)
    c = lax.broadcasted_iota(jnp.int32, (LANES, LANES), 1)
    shift = group.bit_length() - 1
    same = lax.shift_right_logical(r, shift) == lax.shift_right_logical(c, shift)
    return jnp.where(same, 1.0, 0.0).astype(BF16)


def _relayout_kernel(src_ref, *dst_refs, plans):
    x = src_ref[0]
    for dst_ref, moves in zip(dst_refs, plans):
        covered = 0
        for d0, s0, w in sorted(moves):
            if d0 > covered:
                dst_ref[0, :, covered:d0] = jnp.zeros((x.shape[0], d0 - covered), BF16)
            dst_ref[0, :, d0:d0 + w] = x[:, s0:s0 + w].astype(BF16)
            covered = d0 + w
        width = dst_ref.shape[-1]
        if covered < width:
            dst_ref[0, :, covered:] = jnp.zeros((x.shape[0], width - covered), BF16)


def _relayout(w, plans, widths, rows):
    L, R, C = w.shape
    return pl.pallas_call(
        functools.partial(_relayout_kernel, plans=plans),
        grid=(L, R // rows),
        in_specs=[pl.BlockSpec((1, rows, C), lambda l, i: (l, i, 0))],
        out_specs=[pl.BlockSpec((1, rows, n), lambda l, i: (l, i, 0)) for n in widths],
        out_shape=[jax.ShapeDtypeStruct((L, R, n), BF16) for n in widths],
        compiler_params=_cp("parallel", "parallel"),
        name="weight_relayout",
    )(w)


def _row_relayout_kernel(src_ref, dst_ref, *, moves):
    covered = 0
    cols = dst_ref.shape[-1]
    for d0, s0, n in sorted(moves):
        if d0 > covered:
            dst_ref[0, covered:d0, :] = jnp.zeros((d0 - covered, cols), BF16)
        dst_ref[0, d0:d0 + n, :] = src_ref[0, s0:s0 + n, :].astype(BF16)
        covered = d0 + n
    if covered < dst_ref.shape[1]:
        dst_ref[0, covered:, :] = jnp.zeros((dst_ref.shape[1] - covered, cols), BF16)


def _row_relayout(wt, moves, rows_out, cols):
    L, R, C = wt.shape
    return pl.pallas_call(
        functools.partial(_row_relayout_kernel, moves=moves),
        grid=(L, C // cols),
        in_specs=[pl.BlockSpec((1, R, cols), lambda l, i: (l, 0, i))],
        out_specs=pl.BlockSpec((1, rows_out, cols), lambda l, i: (l, 0, i)),
        out_shape=jax.ShapeDtypeStruct((L, rows_out, C), BF16),
        compiler_params=_cp("parallel", "parallel"),
        name="weight_row_relayout",
    )(wt)


def _ada_kernel(ct_ref, w_ref, b_ref, o_ref):
    c = ct_ref[...]
    cond = c / (1.0 + jnp.exp(-c))
    w = w_ref[0]
    for b in range(c.shape[1]):
        o_ref[0, b:b + 1, :] = jnp.sum(w * cond[:, b:b + 1], axis=0, keepdims=True) + b_ref[0]


def _ada_mod(c, w_ada, b_ada):
    L, D, N = w_ada.shape
    B = c.shape[0]
    tn = 1024
    return pl.pallas_call(
        _ada_kernel,
        grid=(L, N // tn),
        in_specs=[pl.BlockSpec((D, B), lambda l, j: (0, 0)),
                  pl.BlockSpec((1, D, tn), lambda l, j: (l, 0, j)),
                  pl.BlockSpec((1, 1, tn), lambda l, j: (l, 0, j))],
        out_specs=pl.BlockSpec((1, B, tn), lambda l, j: (l, 0, j)),
        out_shape=jax.ShapeDtypeStruct((L, B, N), F32),
        compiler_params=_cp("parallel", "parallel"),
        name="ada_mod",
    )(c.T, w_ada, b_ada.reshape(L, 1, N))


def _rope_table_kernel(pos_ref, inv_ref, c_ref, s1_ref, s2_ref):
    ang = pos_ref[...] * inv_ref[...]
    lane = lax.broadcasted_iota(jnp.int32, ang.shape, 1)
    half = MLA_ROPE // 2
    cos = jnp.cos(ang)
    sin = jnp.sin(ang)
    c_ref[...] = jnp.where(lane < MLA_ROPE, cos, 0.0)
    s1_ref[...] = jnp.where(lane < half, -sin, 0.0)
    s2_ref[...] = jnp.where((lane >= half) & (lane < MLA_ROPE), sin, 0.0)


def _rope_tables(pos_col):
    T = pos_col.shape[0]
    half = MLA_ROPE // 2
    inv = ROPE_THETA ** (-jnp.arange(half, dtype=F32) / half)
    inv_row = jnp.concatenate([inv, inv, jnp.zeros((LANES - MLA_ROPE,), F32)]).reshape(1, LANES)
    tm = min(T, 2048)
    sds = jax.ShapeDtypeStruct((T, LANES), F32)
    spec = pl.BlockSpec((tm, LANES), lambda i: (i, 0))
    return pl.pallas_call(
        _rope_table_kernel,
        grid=(T // tm,),
        in_specs=[pl.BlockSpec((tm, 1), lambda i: (i, 0)),
                  pl.BlockSpec((1, LANES), lambda i: (0, 0))],
        out_specs=[spec, spec, spec],
        out_shape=[sds, sds, sds],
        compiler_params=_cp("parallel"),
        name="rope_tables",
    )(pos_col, inv_row)


def _norm_matmul_kernel(x_ref, sh_ref, sc_ref, w_ref, o_ref, h_scr):
    @pl.when(pl.program_id(1) == 0)
    def _():
        for r in range(0, x_ref.shape[0], NORM_ROWS):
            x = x_ref[r:r + NORM_ROWS, :]
            h = x * _rms_scale(x, x.shape[-1]) * (1.0 + sc_ref[0]) + sh_ref[0]
            h_scr[r:r + NORM_ROWS, :] = h.astype(BF16)

    o_ref[...] = _dot_nt(h_scr[...], w_ref[...]).astype(o_ref.dtype)


def _norm_matmul(x2, mod_l, which, wt, l, S, tm, tn):
    T, D = x2.shape
    N = wt.shape[1]
    tpb = S // tm
    w_mode = dict(pipeline_mode=pl.Buffered(1)) if tn == N else {}
    return pl.pallas_call(
        _norm_matmul_kernel,
        grid=(T // tm, N // tn),
        in_specs=[pl.BlockSpec((tm, D), lambda i, j: (i, 0)),
                  pl.BlockSpec((1, 1, D), lambda i, j: (i // tpb, 0, which)),
                  pl.BlockSpec((1, 1, D), lambda i, j: (i // tpb, 0, which + 1)),
                  pl.BlockSpec((None, tn, D), lambda i, j: (l, j, 0), **w_mode)],
        out_specs=pl.BlockSpec((tm, tn), lambda i, j: (i, j)),
        out_shape=jax.ShapeDtypeStruct((T, N), BF16),
        scratch_shapes=[pltpu.VMEM((tm, D), BF16)],
        compiler_params=_cp("parallel", "arbitrary"),
        name="norm_matmul",
    )(x2, mod_l, mod_l, wt)


def _da_prep_kernel(zq_ref, zk_ref, zv_ref, d_ref, gq_ref, gk_ref, qc_ref, kc_ref, q_ref, k_ref, vt_ref):
    lane = lax.broadcasted_iota(jnp.int32, (1, LANES), 1)
    same_map = _same_group_matrix(DA_QK)

    def qk_norm(z, g):
        ss = _dot((z * z).astype(BF16), same_map)
        return z * lax.rsqrt(ss * (1.0 / DA_QK) + EPS) * g

    d = d_ref[...]
    d_hi = jnp.floor(d * (1.0 / LANES)) * LANES
    d_lo = d - d_hi
    k_pos = jnp.where(lane < 3, d_hi, jnp.where(lane < 6, d_lo, 0.0))
    q_pos = jnp.where((lane >= 6) & (lane < 9), -d_hi, jnp.where((lane >= 9) & (lane < 12), -d_lo, 0.0))
    for h in range(DA_HEADS):
        sl = slice(h * LANES, (h + 1) * LANES)
        qn = qk_norm(zq_ref[:, sl].astype(F32), gq_ref[...])
        kn = qk_norm(zk_ref[:, sl].astype(F32), gk_ref[...])
        qa = q_pos + qc_ref[h]
        q_ref[0, h, :LANES, :] = qn.T.astype(BF16)
        q_ref[0, h, LANES:, :] = qa.T.astype(BF16)
        k_ref[0, h, :, :LANES] = kn.astype(BF16)
        k_ref[0, h, :, LANES:] = (k_pos + kc_ref[h]).astype(BF16)
        vt_ref[0, h] = zv_ref[:, sl].astype(F32).T.astype(BF16)


def _da_prep(z, d_col, gq, gk, qc, kc, B, S, tm):
    T = z.shape[0]
    tpb = S // tm
    H = DA_HEADS
    return pl.pallas_call(
        _da_prep_kernel,
        grid=(T // tm,),
        in_specs=[pl.BlockSpec((tm, DA_WIDTH), lambda i: (i, C_DAQ // DA_WIDTH)),
                  pl.BlockSpec((tm, DA_WIDTH), lambda i: (i, C_DAK // DA_WIDTH)),
                  pl.BlockSpec((tm, DA_WIDTH), lambda i: (i, C_DAV // DA_WIDTH)),
                  pl.BlockSpec((tm, 1), lambda i: (i, 0)),
                  pl.BlockSpec((1, LANES), lambda i: (0, 0)),
                  pl.BlockSpec((1, LANES), lambda i: (0, 0)),
                  pl.BlockSpec((H, 1, LANES), lambda i: (0, 0, 0)),
                  pl.BlockSpec((H, 1, LANES), lambda i: (0, 0, 0))],
        out_specs=[pl.BlockSpec((1, H, QK_PAD, tm), lambda i: (i // tpb, 0, 0, i % tpb)),
                   pl.BlockSpec((1, H, tm, QK_PAD), lambda i: (i // tpb, 0, i % tpb, 0)),
                   pl.BlockSpec((1, H, DA_V, tm), lambda i: (i // tpb, 0, 0, i % tpb))],
        out_shape=[jax.ShapeDtypeStruct((B, H, QK_PAD, S), BF16),
                   jax.ShapeDtypeStruct((B, H, S, QK_PAD), BF16),
                   jax.ShapeDtypeStruct((B, H, DA_V, S), BF16)],
        compiler_params=_cp("parallel"),
        name="da_prep",
    )(z, z, z, d_col, gq, gk, qc, kc)


def _mla_prep_kernel(zqa_ref, zkva_ref, zkr_ref, gqa_ref, gkva_ref, wqn_ref, wqr_ref, wkn_ref, wv_ref,
                     gqn_ref, gqr_ref, gkn_ref, gkr_ref, qb_ref, kb_ref, c_ref, s1_ref, s2_ref,
                     q_ref, k_ref, v_ref):
    half = MLA_ROPE // 2
    width = MLA_NOPE + MLA_ROPE
    cos, sin1, sin2 = c_ref[...], s1_ref[...], s2_ref[...]

    def rope(t):
        return (t * cos + pltpu.roll(t, LANES - half, 1) * sin1 + pltpu.roll(t, half, 1) * sin2)

    zqa = zqa_ref[...].astype(F32)
    qa = (zqa * _rms_scale(zqa, MLA_Q_RANK) * gqa_ref[...]).astype(BF16)
    zkva = zkva_ref[...].astype(F32)
    kva = (zkva * _rms_scale(zkva, MLA_KV_RANK) * gkva_ref[...]).astype(BF16)
    kr = zkr_ref[...].astype(F32)
    kr_sq = kr * kr
    kr_rot = rope(kr * gkr_ref[...])
    ones = _same_group_matrix(LANES)
    for h in range(MLA_HEADS):
        sl = slice((h % 2) * LANES, (h % 2 + 1) * LANES)
        if h % 2 == 0:
            cols = slice(h * LANES, (h + 2) * LANES)
            qn_all = _dot(qa, wqn_ref[:, cols])
            qr_all = _dot(qa, wqr_ref[:, cols])
            kn_all = _dot(kva, wkn_ref[:, cols])
            v_all = _dot(kva, wv_ref[:, cols])
        qn, qr, kn = qn_all[:, sl], qr_all[:, sl], kn_all[:, sl]
        q_inv = lax.rsqrt(_dot((qn * qn + qr * qr).astype(BF16), ones) * (1.0 / width) + EPS)
        k_inv = lax.rsqrt(_dot((kn * kn + kr_sq).astype(BF16), ones) * (1.0 / width) + EPS)
        q_ref[0, h, :LANES, :] = (qn * q_inv * gqn_ref[...]).T.astype(BF16)
        q_ref[0, h, LANES:, :] = (rope(qr * q_inv * gqr_ref[...]) + qb_ref[...]).T.astype(BF16)
        k_ref[0, h, :, :LANES] = (kn * k_inv * gkn_ref[...]).astype(BF16)
        k_ref[0, h, :, LANES:] = (kr_rot * k_inv + kb_ref[...]).astype(BF16)
        v_ref[0, h] = v_all[:, sl].T.astype(BF16)


def _mla_prep(z, gqa, gkva, wqn, wqr, wkn, wv, l, gqn, gqr, gkn, gkr, qb, kb, tabs, B, S, tm):
    T = z.shape[0]
    tpb = S // tm
    H = MLA_HEADS
    full = lambda a: pl.BlockSpec(a.shape, lambda i: (0,) * a.ndim)
    layer = lambda a: pl.BlockSpec((None,) + a.shape[1:], lambda i: (l, 0, 0))
    row = pl.BlockSpec((tm, LANES), lambda i: (i, 0))
    return pl.pallas_call(
        _mla_prep_kernel,
        grid=(T // tm,),
        in_specs=[pl.BlockSpec((tm, MLA_Q_RANK), lambda i: (i, C_QA // MLA_Q_RANK)),
                  pl.BlockSpec((tm, MLA_KV_RANK), lambda i: (i, C_KVA // MLA_KV_RANK)),
                  pl.BlockSpec((tm, LANES), lambda i: (i, C_KR // LANES)),
                  full(gqa), full(gkva), layer(wqn), layer(wqr), layer(wkn), layer(wv),
                  full(gqn), full(gqr), full(gkn), full(gkr), full(qb), full(kb), row, row, row],
        out_specs=[pl.BlockSpec((1, H, QK_PAD, tm), lambda i: (i // tpb, 0, 0, i % tpb)),
                   pl.BlockSpec((1, H, tm, QK_PAD), lambda i: (i // tpb, 0, i % tpb, 0)),
                   pl.BlockSpec((1, H, MLA_V, tm), lambda i: (i // tpb, 0, 0, i % tpb))],
        out_shape=[jax.ShapeDtypeStruct((B, H, QK_PAD, S), BF16),
                   jax.ShapeDtypeStruct((B, H, S, QK_PAD), BF16),
                   jax.ShapeDtypeStruct((B, H, MLA_V, S), BF16)],
        compiler_params=_cp("parallel"),
        name="mla_prep",
    )(z, z, z, gqa, gkva, wqn, wqr, wkn, wv, gqn, gqr, gkn, gkr, qb, kb, *tabs)


def _gelu_tanh(x):
    return 0.5 * x * (1.0 + jnp.tanh(0.7978845608028654 * (x + 0.044715 * (x * x * x))))


def _sg_kernel(zu_ref, zv_ref, g_ref, w_ref, b_ref, o_ref):
    tm = zu_ref.shape[0]
    r = lax.broadcasted_iota(jnp.int32, (SG_LEN, SG_LEN), 0)
    c = lax.broadcasted_iota(jnp.int32, (SG_LEN, SG_LEN), 1)
    causal = c <= r
    for g in range(SG_GROUPS):
        sl = slice(g * SG_CH, (g + 1) * SG_CH)
        wt = jnp.where(causal, w_ref[g], 0.0).astype(BF16)
        v = _gelu_tanh(zv_ref[:, sl].astype(F32))
        v = (v * _rms_scale(v, SG_CH) * g_ref[:, sl]).astype(BF16)
        u = _gelu_tanh(zu_ref[:, sl].astype(F32))
        for n in range(tm // SG_LEN):
            rows = slice(n * SG_LEN, (n + 1) * SG_LEN)
            s = _dot(wt, v[rows, :]) + b_ref[:, sl]
            o_ref[rows, sl] = (u[rows, :] * s).astype(BF16)


def _sg(z, gain, w, bias, tm):
    T = z.shape[0]
    return pl.pallas_call(
        _sg_kernel,
        grid=(T // tm,),
        in_specs=[pl.BlockSpec((tm, SG_WIDTH), lambda i: (i, C_SGU // SG_WIDTH)),
                  pl.BlockSpec((tm, SG_WIDTH), lambda i: (i, C_SGV // SG_WIDTH)),
                  pl.BlockSpec((1, SG_WIDTH), lambda i: (0, 0)),
                  pl.BlockSpec((SG_GROUPS, SG_LEN, SG_LEN), lambda i: (0, 0, 0)),
                  pl.BlockSpec((SG_LEN, SG_WIDTH), lambda i: (0, 0))],
        out_specs=pl.BlockSpec((tm, SG_WIDTH), lambda i: (i, 0)),
        out_shape=jax.ShapeDtypeStruct((T, SG_WIDTH), BF16),
        compiler_params=_cp("parallel"),
        name="spatial_gating",
    )(z, z, gain, w, bias)


def _flash_t(qt, k_ref, vt_ref, n_off, diag_fixes, online):
    n = qt.shape[1]

    def step(j, state, fix=None):
        m, l, acc = state
        start = pl.multiple_of(j * KEY_TILE, KEY_TILE)
        s = _dot(k_ref[0, 0, pl.ds(start, KEY_TILE), :], qt)
        if fix is not None:
            s = fix(s)
        if online:
            m_new = jnp.maximum(m, jnp.max(s, axis=0, keepdims=True))
            alpha = jnp.exp2(m - m_new)
            p = jnp.exp2(s - m_new)
            l = alpha * l
            acc = alpha * acc
        else:
            m_new = m
            p = jnp.exp2(s)
        l = l + jnp.sum(p, axis=0, keepdims=True)
        acc = acc + _dot(vt_ref[0, 0, :, pl.ds(start, KEY_TILE)], p.astype(BF16))
        return m_new, l, acc

    def run(j0, count, state):
        for u in range(count):
            state = step(j0 + u, state)
        return state

    state = (jnp.full((1, n), NEG_BIG, F32), jnp.zeros((1, n), F32), jnp.zeros((DA_V, n), F32))
    n_grp = n_off // KV_UNROLL
    state = lax.fori_loop(0, n_grp, lambda g, st: run(g * KV_UNROLL, KV_UNROLL, st), state)
    state = lax.fori_loop(n_grp * KV_UNROLL, n_off, lambda j, st: run(j, 1, st), state)
    for t, fix in enumerate(diag_fixes):
        state = step(n_off + t, state, fix)
    _, l, acc = state
    return acc / l


def _chunk_visible(n, tq, key_off):
    r = lax.broadcasted_iota(jnp.int32, (KEY_TILE, n), 0) + key_off
    c = lax.broadcasted_iota(jnp.int32, (KEY_TILE, n), 1)
    if n > tq:
        c = jnp.where(c >= tq, c - tq, c)
    shift = CHUNK.bit_length() - 1
    return lax.shift_right_logical(r, shift) <= lax.shift_right_logical(c, shift)


def _da_attn_kernel(q_ref, k_ref, vt_ref, dk_ref, dq_ref, sl2_ref, lam_ref, hg_ref, o_ref, *,
                    tq, lam_init, online):
    qi = pl.program_id(2)
    qf = q_ref[0, 0]
    row = lax.broadcasted_iota(jnp.int32, qf.shape, 0)
    zero = jnp.zeros_like(qf)
    qt = jnp.concatenate([jnp.where((row >= DA_QK) & (row < 2 * DA_QK), zero, qf),
                          jnp.where(row < DA_QK, zero, qf)], axis=1)

    def fix(s):
        dq = dq_ref[0]
        dq2 = jnp.concatenate([dq, dq], axis=1)
        corr = sl2_ref[0, :, :1] * jnp.maximum(dk_ref[...] - dq2, 0.0)
        return jnp.where(_chunk_visible(2 * tq, tq, 0), s - corr, NEG_BIG)

    ot = _flash_t(qt, k_ref, vt_ref, qi, [fix], online)
    lv = lam_ref[...]
    lam = (jnp.exp(jnp.sum(lv[0:1] * lv[1:2], axis=-1, keepdims=True))
           - jnp.exp(jnp.sum(lv[2:3] * lv[3:4], axis=-1, keepdims=True)) + lam_init)
    od = ot[:, :tq] - lam * ot[:, tq:]
    inv = lax.rsqrt(jnp.sum(od * od, axis=0, keepdims=True) * (1.0 / DA_V) + EPS)
    od = od * inv * (1.0 - lam_init)
    o_ref[...] = (od.T * hg_ref[0]).astype(o_ref.dtype)


def _da_attention(q, k, vt, d_col, d_row, sl2, lam_vecs, hg, B, S, tq, lam_init, online):
    H = DA_HEADS
    nq = S // tq
    assert tq == KEY_TILE
    return pl.pallas_call(
        functools.partial(_da_attn_kernel, tq=tq, lam_init=lam_init, online=online),
        grid=(B, H, nq),
        in_specs=[pl.BlockSpec((1, 1, QK_PAD, tq), lambda b, h, i: (b, h, 0, i)),
                  pl.BlockSpec((1, 1, S, QK_PAD), lambda b, h, i: (b, h, 0, 0)),
                  pl.BlockSpec((1, 1, DA_V, S), lambda b, h, i: (b, h, 0, 0)),
                  pl.BlockSpec((tq, 1), lambda b, h, i: (b * nq + i, 0)),
                  pl.BlockSpec((1, 1, tq), lambda b, h, i: (b, 0, i)),
                  pl.BlockSpec((1, 1, LANES), lambda b, h, i: (h, 0, 0)),
                  pl.BlockSpec((4, DA_QK), lambda b, h, i: (0, 0)),
                  pl.BlockSpec((1, 1, DA_V), lambda b, h, i: (h, 0, 0))],
        out_specs=pl.BlockSpec((tq, DA_V), lambda b, h, i: (b * nq + i, h)),
        out_shape=jax.ShapeDtypeStruct((B * S, DA_WIDTH), BF16),
        compiler_params=_cp("parallel", "parallel", "parallel"),
        name="da_attention",
    )(q, k, vt, d_col, d_row, sl2, lam_vecs, hg)


def _mla_attn_kernel(q_ref, k_ref, vt_ref, o_ref, *, tq, online):
    qi = pl.program_id(2)
    tiles = tq // KEY_TILE
    fixes = [lambda s, t=t: jnp.where(_chunk_visible(tq, tq, t * KEY_TILE), s, NEG_BIG) for t in range(tiles)]
    ot = _flash_t(q_ref[0, 0], k_ref, vt_ref, qi * tiles, fixes, online)
    o_ref[...] = ot.T.astype(o_ref.dtype)


def _mla_attention(q, k, vt, B, S, tq, online):
    H = MLA_HEADS
    nq = S // tq
    return pl.pallas_call(
        functools.partial(_mla_attn_kernel, tq=tq, online=online),
        grid=(B, H, nq),
        in_specs=[pl.BlockSpec((1, 1, QK_PAD, tq), lambda b, h, i: (b, h, 0, i)),
                  pl.BlockSpec((1, 1, S, QK_PAD), lambda b, h, i: (b, h, 0, 0)),
                  pl.BlockSpec((1, 1, MLA_V, S), lambda b, h, i: (b, h, 0, 0))],
        out_specs=pl.BlockSpec((tq, MLA_V), lambda b, h, i: (b * nq + i, h)),
        out_shape=jax.ShapeDtypeStruct((B * S, MLA_WIDTH), BF16),
        compiler_params=_cp("parallel", "parallel", "parallel"),
        name="mla_attention",
    )(q, k, vt)


def _out_proj_kernel(a_ref, b_ref, c_ref, w_ref, x_ref, g_ref, o_ref):
    ka, kb = a_ref.shape[1], a_ref.shape[1] + b_ref.shape[1]
    mix = (_dot(a_ref[...], w_ref[:ka, :].astype(BF16)) + _dot(b_ref[...], w_ref[ka:kb, :].astype(BF16))
           + _dot(c_ref[...], w_ref[kb:, :].astype(BF16)))
    o_ref[...] = x_ref[...] + g_ref[0] * mix


def _out_proj(a, b, c, w, l, x2, mod_l, S, tm, tn):
    T, D = x2.shape
    tpb = S // tm
    nb = D // tn
    w_mode = dict(pipeline_mode=pl.Buffered(1)) if nb == 1 else {}
    return pl.pallas_call(
        _out_proj_kernel,
        grid=(T // tm, nb),
        in_specs=[pl.BlockSpec((tm, DA_WIDTH), lambda i, j: (i, 0)),
                  pl.BlockSpec((tm, MLA_WIDTH), lambda i, j: (i, 0)),
                  pl.BlockSpec((tm, SG_WIDTH), lambda i, j: (i, 0)),
                  pl.BlockSpec((None, w.shape[1], tn), lambda i, j: (l, 0, j), **w_mode),
                  pl.BlockSpec((tm, tn), lambda i, j: (i, j)),
                  pl.BlockSpec((1, 1, tn), lambda i, j: (i // tpb, 0, 2 * nb + j))],
        out_specs=pl.BlockSpec((tm, tn), lambda i, j: (i, j)),
        out_shape=jax.ShapeDtypeStruct((T, D), F32),
        compiler_params=_cp("parallel", "parallel"),
        name="out_proj",
    )(a, b, c, w, x2, mod_l)


def _ffn_kernel(x_ref, xh_ref, sh_ref, sc_ref, g_ref, wg_ref, wv_ref, cwg_ref, cwv_ref, cbg_ref, cbv_ref,
                wd_ref, o_ref, h_scr, *, tpb):
    i = pl.program_id(0)
    j = pl.program_id(1)

    @pl.when(j == 0)
    def _():
        def modnorm(x):
            return x * _rms_scale(x, x.shape[-1]) * (1.0 + sc_ref[0]) + sh_ref[0]

        for r in range(0, x_ref.shape[0], NORM_ROWS):
            h_scr[HALO + r:HALO + r + NORM_ROWS, :] = modnorm(x_ref[r:r + NORM_ROWS, :]).astype(BF16)
        keep = jnp.where(i % tpb == 0, 0.0, 1.0)
        h_scr[:HALO, :] = (modnorm(xh_ref[...]) * keep).astype(BF16)
        o_ref[...] = jnp.zeros_like(o_ref)

    h = h_scr[...]

    def conv(a, cw_ref, cb_ref):
        y = (cb_ref[...] + a * cw_ref[2:3, :] + pltpu.roll(a, 1, 0) * cw_ref[1:2, :]
             + pltpu.roll(a, 2, 0) * cw_ref[0:1, :])
        return y[HALO:, :]

    yg = conv(_dot(h, wg_ref[...]), cwg_ref, cbg_ref)
    yv = conv(_dot(h, wv_ref[...]), cwv_ref, cbv_ref)
    act = (yg / (1.0 + jnp.exp(-yg)) * yv).astype(BF16)
    o_ref[...] += _dot(act, wd_ref[...].astype(BF16))

    @pl.when(j == pl.num_programs(1) - 1)
    def _():
        o_ref[...] = x_ref[...] + g_ref[0] * o_ref[...]


def _ffn(x2, mod_l, w_up, conv_w, conv_b, w_down, l, S, tm, tf):
    T, D = x2.shape
    F = w_down.shape[1]
    nf = F // tf
    tpb = S // tm
    hb = tm // HALO
    return pl.pallas_call(
        functools.partial(_ffn_kernel, tpb=tpb),
        grid=(T // tm, nf),
        in_specs=[pl.BlockSpec((tm, D), lambda i, j: (i, 0)),
                  pl.BlockSpec((HALO, D), lambda i, j: (jnp.maximum(i * hb - 1, 0), 0)),
                  pl.BlockSpec((1, 1, D), lambda i, j: (i // tpb, 0, 3)),
                  pl.BlockSpec((1, 1, D), lambda i, j: (i // tpb, 0, 4)),
                  pl.BlockSpec((1, 1, D), lambda i, j: (i // tpb, 0, 5)),
                  pl.BlockSpec((None, D, tf), lambda i, j: (l, 0, j)),
                  pl.BlockSpec((None, D, tf), lambda i, j: (l, 0, nf + j)),
                  pl.BlockSpec((None, CONV_W, tf), lambda i, j: (l, 0, j)),
                  pl.BlockSpec((None, CONV_W, tf), lambda i, j: (l, 0, nf + j)),
                  pl.BlockSpec((None, 1, tf), lambda i, j: (l, 0, j)),
                  pl.BlockSpec((None, 1, tf), lambda i, j: (l, 0, nf + j)),
                  pl.BlockSpec((None, tf, D), lambda i, j: (l, j, 0))],
        out_specs=pl.BlockSpec((tm, D), lambda i, j: (i, 0), pipeline_mode=pl.Buffered(1)),
        out_shape=jax.ShapeDtypeStruct((T, D), F32),
        scratch_shapes=[pltpu.VMEM((HALO + tm, D), BF16)],
        compiler_params=_cp("parallel", "arbitrary"),
        name="conv_ffn",
    )(x2, x2, mod_l, mod_l, mod_l, w_up, w_up, conv_w, conv_w, conv_b, conv_b, w_down)


def _split3(v):
    a = v.astype(BF16).astype(F32)
    b = (v - a).astype(BF16).astype(F32)
    c = (v - a - b).astype(BF16).astype(F32)
    return a, b, c


def kernel(x, c, positions, w_ada, b_ada, w_in, da_q_gain, da_k_gain, da_lq1, da_lk1, da_lq2, da_lk2, da_head_gain, mla_q_a_gain, mla_w_uq, mla_kv_a_gain, mla_w_ukv, mla_q_gain, mla_k_gain, sg_v_gain, sg_w, sg_b, w_out, ffn_w_up, ffn_conv_w, ffn_conv_b, ffn_w_down):
    B, S, D = x.shape
    L = w_ada.shape[0]
    T = B * S
    assert S % ROW_TILE == 0 and S % KEY_TILE == 0
    tm_big = min(S, FFN_ROWS)
    tm = ROW_TILE
    tq_mla = min(S, MLA_QUERIES)

    mod = _ada_mod(c, w_ada, b_ada)

    d = (positions - positions[:, :1]).astype(F32)
    d_col = d.reshape(T, 1)
    d_row = d.reshape(B, 1, S)
    tabs = _rope_tables(positions.astype(F32).reshape(T, 1))

    H = DA_HEADS
    sl = jnp.asarray(ALIBI_SLOPES, F32) * LOG2E
    slope6 = jnp.tile(jnp.stack(_split3(sl), axis=-1), (1, 2))
    zeros = lambda n: jnp.zeros((H, n), F32)
    kc = jnp.concatenate([zeros(6), slope6, jnp.ones((H, 3), F32), zeros(LANES - 15)], axis=-1)
    kc = kc.reshape(H, 1, LANES)
    sl2 = jnp.broadcast_to((2.0 * sl)[:, None, None], (H, 1, LANES))

    da_scale = DA_QK ** -0.5 * LOG2E
    mla_width = MLA_NOPE + MLA_ROPE
    mla_scale = mla_width ** -0.5 * LOG2E
    pad_r = lambda g: jnp.pad(g, (0, LANES - MLA_ROPE)).reshape(1, LANES)
    shift_lanes = lambda v: jnp.pad(v, (MLA_ROPE, LANES - MLA_ROPE - 3)).reshape(1, LANES)
    kb_mla = shift_lanes(jnp.ones((3,), F32))
    gmax = lambda g: jnp.max(jnp.abs(g))

    in_moves = [(C_QA, 2304, 512), (C_SGU, 3136, 512), (C_SGV, 3648, 512), (C_DAQ, 0, 768),
                (C_DAK, 768, 768), (C_DAV, 1536, 768), (C_KVA, 2816, 256), (C_KR, 3072, MLA_ROPE)]
    w_in_p = _row_relayout(jnp.swapaxes(w_in, 1, 2), in_moves, IN_PAD, 512)
    heads = range(MLA_HEADS)
    wqn_all, wqr_all = _relayout(
        mla_w_uq,
        [[(h * LANES, h * mla_width, MLA_NOPE) for h in heads],
         [(h * LANES, h * mla_width + MLA_NOPE, MLA_ROPE) for h in heads]],
        [MLA_WIDTH, MLA_HEADS * LANES], 256)
    kv_width = MLA_NOPE + MLA_V
    wkn_all, wv_all = _relayout(
        mla_w_ukv,
        [[(h * LANES, h * kv_width, MLA_NOPE) for h in heads],
         [(h * LANES, h * kv_width + MLA_NOPE, MLA_V) for h in heads]],
        [MLA_WIDTH, MLA_WIDTH], 256)
    w_up_b = ffn_w_up.astype(BF16)
    conv_b3 = ffn_conv_b.reshape(L, 1, -1)

    x2 = x.reshape(T, D)
    for l in range(L):
        mod_l = mod[l].reshape(B, 1, 6 * D)
        z = _norm_matmul(x2, mod_l, 0, w_in_p, l, S, tm, IN_PAD)

        lam_init = 0.8 - 0.6 * math.exp(-0.3 * l)
        gq = (jnp.tile(da_q_gain[l], 2) * da_scale).reshape(1, LANES)
        gk = jnp.tile(da_k_gain[l], 2).reshape(1, LANES)
        bound = BOUND_SLACK * DA_QK * da_scale * gmax(da_q_gain[l]) * gmax(da_k_gain[l])
        nb = jnp.broadcast_to(-jnp.stack(_split3(bound)), (H, 3))
        qc = jnp.concatenate([slope6, zeros(6), nb, zeros(LANES - 15)], axis=-1).reshape(H, 1, LANES)
        q_da, k_da, vt_da = _da_prep(z, d_col, gq, gk, qc, kc, B, S, tm)
        lam_vecs = jnp.stack([da_lq1[l], da_lk1[l], da_lq2[l], da_lk2[l]])
        da_args = (q_da, k_da, vt_da, d_col, d_row, sl2, lam_vecs, da_head_gain[l].reshape(H, 1, DA_V))
        out_a = lax.cond(
            2.0 * bound <= MAX_SHIFT,
            lambda a: _da_attention(*a, B, S, KEY_TILE, lam_init, online=False),
            lambda a: _da_attention(*a, B, S, KEY_TILE, lam_init, online=True), da_args)

        qg, kg = mla_q_gain[l], mla_k_gain[l]
        bound_m = BOUND_SLACK * mla_width * mla_scale * gmax(qg) * gmax(kg)
        q_m, k_m, v_m = _mla_prep(
            z, mla_q_a_gain[l].reshape(1, -1), mla_kv_a_gain[l].reshape(1, -1),
            wqn_all, wqr_all, wkn_all, wv_all, l,
            (qg[:MLA_NOPE] * mla_scale).reshape(1, LANES), pad_r(qg[MLA_NOPE:] * mla_scale),
            kg[:MLA_NOPE].reshape(1, LANES), pad_r(kg[MLA_NOPE:]),
            shift_lanes(-jnp.stack(_split3(bound_m))), kb_mla, tabs, B, S, tm)
        out_b = lax.cond(
            2.0 * bound_m <= MAX_SHIFT,
            lambda a: _mla_attention(*a, B, S, tq_mla, online=False),
            lambda a: _mla_attention(*a, B, S, tq_mla, online=True), (q_m, k_m, v_m))

        sg_bias = jnp.repeat(sg_b[l].T, SG_CH, axis=1)
        out_c = _sg(z, sg_v_gain[l].reshape(1, SG_WIDTH), sg_w[l], sg_bias, tm)

        x2 = _out_proj(out_a, out_b, out_c, w_out, l, x2, mod_l, S, tm, D)
        x2 = _ffn(x2, mod_l, w_up_b, ffn_conv_w, conv_b3, ffn_w_down, l, S, tm_big, FFN_TILE)
    return x2.reshape(B, S, D)
```

```python
import functools
import math

import jax
import jax.numpy as jnp
from jax import lax
from jax.experimental import pallas as pl
from jax.experimental.pallas import tpu as pltpu

CHUNK = 64
EPS = 1e-6
DA_HEADS = 6
DA_QK = 64
DA_V = 128
MLA_HEADS = 6
MLA_Q_RANK = 512
MLA_KV_RANK = 256
MLA_NOPE = 128
MLA_ROPE = 64
MLA_V = 128
ROPE_THETA = 10000.0
SG_GROUPS = 4
SG_CH = 128
SG_LEN = 128
CONV_W = 3
DA_WIDTH = DA_HEADS * DA_V
MLA_WIDTH = MLA_HEADS * MLA_V
SG_WIDTH = SG_GROUPS * SG_CH
ALIBI_SLOPES = tuple(2.0 ** (-8.0 * (h + 1) / DA_HEADS) for h in range(DA_HEADS))

LOG2E = 1.4426950408889634
LANES = 128
QK_PAD = 256
NEG_BIG = -1e30
MAX_SHIFT = 100.0
BOUND_SLACK = 1.02
KV_UNROLL = 4
KEY_TILE = 512
FFN_TILE = 512
FFN_ROWS = 1024
DOWN_COLS = 512
DOWN_CAST_ROWS = 512
ROW_TILE = 512
MLA_QUERIES = 1024
HALO = 16
NORM_ROWS = 256
VMEM_LIMIT = 56 * 1024 * 1024

C_QA, C_SGU, C_SGV, C_DAQ, C_DAK, C_DAV, C_KVA, C_KR, IN_PAD = (
    0, 512, 1024, 1536, 2304, 3072, 3840, 4096, 4224)

BF16 = jnp.bfloat16
F32 = jnp.float32


def _cp(*sem):
    return pltpu.CompilerParams(dimension_semantics=sem, vmem_limit_bytes=VMEM_LIMIT)


def _dot(a, b):
    return jnp.dot(a, b, preferred_element_type=F32)


def _dot_nt(a, b):
    return lax.dot_general(a, b, (((1,), (1,)), ((), ())), preferred_element_type=F32)


def _rms_scale(x, n):
    return lax.rsqrt(jnp.sum(x * x, axis=-1, keepdims=True) * (1.0 / n) + EPS)


def _same_group_matrix(group):
    r = lax.broadcasted_iota(jnp.int32, (LANES, LANES), 0)
    c = lax.broadcasted_iota(jnp.int32, (LANES, LANES), 1)
    shift = group.bit_length() - 1
    same = lax.shift_right_logical(r, shift) == lax.shift_right_logical(c, shift)
    return jnp.where(same, 1.0, 0.0).astype(BF16)


def _relayout_kernel(src_ref, *dst_refs, plans):
    x = src_ref[0]
    for dst_ref, moves in zip(dst_refs, plans):
        covered = 0
        for d0, s0, w in sorted(moves):
            if d0 > covered:
                dst_ref[0, :, covered:d0] = jnp.zeros((x.shape[0], d0 - covered), BF16)
            dst_ref[0, :, d0:d0 + w] = x[:, s0:s0 + w].astype(BF16)
            covered = d0 + w
        width = dst_ref.shape[-1]
        if covered < width:
            dst_ref[0, :, covered:] = jnp.zeros((x.shape[0], width - covered), BF16)


def _relayout(w, plans, widths, rows):
    L, R, C = w.shape
    return pl.pallas_call(
        functools.partial(_relayout_kernel, plans=plans),
        grid=(L, R // rows),
        in_specs=[pl.BlockSpec((1, rows, C), lambda l, i: (l, i, 0))],
        out_specs=[pl.BlockSpec((1, rows, n), lambda l, i: (l, i, 0)) for n in widths],
        out_shape=[jax.ShapeDtypeStruct((L, R, n), BF16) for n in widths],
        compiler_params=_cp("parallel", "parallel"),
        name="weight_relayout",
    )(w)


def _row_relayout_kernel(src_ref, dst_ref, *, moves):
    covered = 0
    cols = dst_ref.shape[-1]
    for d0, s0, n in sorted(moves):
        if d0 > covered:
            dst_ref[0, covered:d0, :] = jnp.zeros((d0 - covered, cols), BF16)
        dst_ref[0, d0:d0 + n, :] = src_ref[0, s0:s0 + n, :].astype(BF16)
        covered = d0 + n
    if covered < dst_ref.shape[1]:
        dst_ref[0, covered:, :] = jnp.zeros((dst_ref.shape[1] - covered, cols), BF16)


def _row_relayout(wt, moves, rows_out, cols):
    L, R, C = wt.shape
    return pl.pallas_call(
        functools.partial(_row_relayout_kernel, moves=moves),
        grid=(L, C // cols),
        in_specs=[pl.BlockSpec((1, R, cols), lambda l, i: (l, 0, i))],
        out_specs=pl.BlockSpec((1, rows_out, cols), lambda l, i: (l, 0, i)),
        out_shape=jax.ShapeDtypeStruct((L, rows_out, C), BF16),
        compiler_params=_cp("parallel", "parallel"),
        name="weight_row_relayout",
    )(wt)


def _ada_kernel(ct_ref, w_ref, b_ref, o_ref):
    c = ct_ref[...]
    cond = c / (1.0 + jnp.exp(-c))
    w = w_ref[0]
    for b in range(c.shape[1]):
        o_ref[0, b:b + 1, :] = jnp.sum(w * cond[:, b:b + 1], axis=0, keepdims=True) + b_ref[0]


def _ada_mod(c, w_ada, b_ada):
    L, D, N = w_ada.shape
    B = c.shape[0]
    tn = 1024
    return pl.pallas_call(
        _ada_kernel,
        grid=(L, N // tn),
        in_specs=[pl.BlockSpec((D, B), lambda l, j: (0, 0)),
                  pl.BlockSpec((1, D, tn), lambda l, j: (l, 0, j)),
                  pl.BlockSpec((1, 1, tn), lambda l, j: (l, 0, j))],
        out_specs=pl.BlockSpec((1, B, tn), lambda l, j: (l, 0, j)),
        out_shape=jax.ShapeDtypeStruct((L, B, N), F32),
        compiler_params=_cp("parallel", "parallel"),
        name="ada_mod",
    )(c.T, w_ada, b_ada.reshape(L, 1, N))


def _rope_table_kernel(pos_ref, inv_ref, c_ref, s1_ref, s2_ref):
    ang = pos_ref[...] * inv_ref[...]
    lane = lax.broadcasted_iota(jnp.int32, ang.shape, 1)
    half = MLA_ROPE // 2
    cos = jnp.cos(ang)
    sin = jnp.sin(ang)
    c_ref[...] = jnp.where(lane < MLA_ROPE, cos, 0.0)
    s1_ref[...] = jnp.where(lane < half, -sin, 0.0)
    s2_ref[...] = jnp.where((lane >= half) & (lane < MLA_ROPE), sin, 0.0)


def _rope_tables(pos_col):
    T = pos_col.shape[0]
    half = MLA_ROPE // 2
    inv = ROPE_THETA ** (-jnp.arange(half, dtype=F32) / half)
    inv_row = jnp.concatenate([inv, inv, jnp.zeros((LANES - MLA_ROPE,), F32)]).reshape(1, LANES)
    tm = min(T, 2048)
    sds = jax.ShapeDtypeStruct((T, LANES), F32)
    spec = pl.BlockSpec((tm, LANES), lambda i: (i, 0))
    return pl.pallas_call(
        _rope_table_kernel,
        grid=(T // tm,),
        in_specs=[pl.BlockSpec((tm, 1), lambda i: (i, 0)),
                  pl.BlockSpec((1, LANES), lambda i: (0, 0))],
        out_specs=[spec, spec, spec],
        out_shape=[sds, sds, sds],
        compiler_params=_cp("parallel"),
        name="rope_tables",
    )(pos_col, inv_row)


def _norm_matmul_kernel(x_ref, sh_ref, sc_ref, w_ref, o_ref, h_scr):
    @pl.when(pl.program_id(1) == 0)
    def _():
        for r in range(0, x_ref.shape[0], NORM_ROWS):
            x = x_ref[r:r + NORM_ROWS, :]
            h = x * _rms_scale(x, x.shape[-1]) * (1.0 + sc_ref[0]) + sh_ref[0]
            h_scr[r:r + NORM_ROWS, :] = h.astype(BF16)

    o_ref[...] = _dot_nt(h_scr[...], w_ref[...]).astype(o_ref.dtype)


def _norm_matmul(x2, mod_l, which, wt, l, S, tm, tn):
    T, D = x2.shape
    N = wt.shape[1]
    tpb = S // tm
    w_mode = dict(pipeline_mode=pl.Buffered(1)) if tn == N else {}
    return pl.pallas_call(
        _norm_matmul_kernel,
        grid=(T // tm, N // tn),
        in_specs=[pl.BlockSpec((tm, D), lambda i, j: (i, 0)),
                  pl.BlockSpec((1, 1, D), lambda i, j: (i // tpb, 0, which)),
                  pl.BlockSpec((1, 1, D), lambda i, j: (i // tpb, 0, which + 1)),
                  pl.BlockSpec((None, tn, D), lambda i, j: (l, j, 0), **w_mode)],
        out_specs=pl.BlockSpec((tm, tn), lambda i, j: (i, j)),
        out_shape=jax.ShapeDtypeStruct((T, N), BF16),
        scratch_shapes=[pltpu.VMEM((tm, D), BF16)],
        compiler_params=_cp("parallel", "arbitrary"),
        name="norm_matmul",
    )(x2, mod_l, mod_l, wt)


def _da_prep_kernel(zq_ref, zk_ref, zv_ref, d_ref, gq_ref, gk_ref, qc_ref, kc_ref, q_ref, k_ref, vt_ref):
    lane = lax.broadcasted_iota(jnp.int32, (1, LANES), 1)
    same_map = _same_group_matrix(DA_QK)

    def qk_norm(z, g):
        ss = _dot((z * z).astype(BF16), same_map)
        return z * lax.rsqrt(ss * (1.0 / DA_QK) + EPS) * g

    d = d_ref[...]
    d_hi = jnp.floor(d * (1.0 / LANES)) * LANES
    d_lo = d - d_hi
    k_pos = jnp.where(lane < 3, d_hi, jnp.where(lane < 6, d_lo, 0.0))
    q_pos = jnp.where((lane >= 6) & (lane < 9), -d_hi, jnp.where((lane >= 9) & (lane < 12), -d_lo, 0.0))
    for h in range(DA_HEADS):
        sl = slice(h * LANES, (h + 1) * LANES)
        qn = qk_norm(zq_ref[:, sl].astype(F32), gq_ref[...])
        kn = qk_norm(zk_ref[:, sl].astype(F32), gk_ref[...])
        qa = q_pos + qc_ref[h]
        q_ref[0, h, :LANES, :] = qn.T.astype(BF16)
        q_ref[0, h, LANES:, :] = qa.T.astype(BF16)
        k_ref[0, h, :, :LANES] = kn.astype(BF16)
        k_ref[0, h, :, LANES:] = (k_pos + kc_ref[h]).astype(BF16)
        vt_ref[0, h] = zv_ref[:, sl].astype(F32).T.astype(BF16)


def _da_prep(z, d_col, gq, gk, qc, kc, B, S, tm):
    T = z.shape[0]
    tpb = S // tm
    H = DA_HEADS
    return pl.pallas_call(
        _da_prep_kernel,
        grid=(T // tm,),
        in_specs=[pl.BlockSpec((tm, DA_WIDTH), lambda i: (i, C_DAQ // DA_WIDTH)),
                  pl.BlockSpec((tm, DA_WIDTH), lambda i: (i, C_DAK // DA_WIDTH)),
                  pl.BlockSpec((tm, DA_WIDTH), lambda i: (i, C_DAV // DA_WIDTH)),
                  pl.BlockSpec((tm, 1), lambda i: (i, 0)),
                  pl.BlockSpec((1, LANES), lambda i: (0, 0)),
                  pl.BlockSpec((1, LANES), lambda i: (0, 0)),
                  pl.BlockSpec((H, 1, LANES), lambda i: (0, 0, 0)),
                  pl.BlockSpec((H, 1, LANES), lambda i: (0, 0, 0))],
        out_specs=[pl.BlockSpec((1, H, QK_PAD, tm), lambda i: (i // tpb, 0, 0, i % tpb)),
                   pl.BlockSpec((1, H, tm, QK_PAD), lambda i: (i // tpb, 0, i % tpb, 0)),
                   pl.BlockSpec((1, H, DA_V, tm), lambda i: (i // tpb, 0, 0, i % tpb))],
        out_shape=[jax.ShapeDtypeStruct((B, H, QK_PAD, S), BF16),
                   jax.ShapeDtypeStruct((B, H, S, QK_PAD), BF16),
                   jax.ShapeDtypeStruct((B, H, DA_V, S), BF16)],
        compiler_params=_cp("parallel"),
        name="da_prep",
    )(z, z, z, d_col, gq, gk, qc, kc)


def _mla_prep_kernel(zqa_ref, zkva_ref, zkr_ref, gqa_ref, gkva_ref, wqn_ref, wqr_ref, wkn_ref, wv_ref,
                     gqn_ref, gqr_ref, gkn_ref, gkr_ref, qb_ref, kb_ref, c_ref, s1_ref, s2_ref,
                     q_ref, k_ref, v_ref):
    half = MLA_ROPE // 2
    width = MLA_NOPE + MLA_ROPE
    cos, sin1, sin2 = c_ref[...], s1_ref[...], s2_ref[...]

    def rope(t):
        return (t * cos + pltpu.roll(t, LANES - half, 1) * sin1 + pltpu.roll(t, half, 1) * sin2)

    zqa = zqa_ref[...].astype(F32)
    qa = (zqa * _rms_scale(zqa, MLA_Q_RANK) * gqa_ref[...]).astype(BF16)
    zkva = zkva_ref[...].astype(F32)
    kva = (zkva * _rms_scale(zkva, MLA_KV_RANK) * gkva_ref[...]).astype(BF16)
    kr = zkr_ref[...].astype(F32)
    kr_sq = kr * kr
    kr_rot = rope(kr * gkr_ref[...])
    ones = _same_group_matrix(LANES)
    for h in range(MLA_HEADS):
        sl = slice((h % 2) * LANES, (h % 2 + 1) * LANES)
        if h % 2 == 0:
            cols = slice(h * LANES, (h + 2) * LANES)
            qn_all = _dot(qa, wqn_ref[:, cols])
            qr_all = _dot(qa, wqr_ref[:, cols])
            kn_all = _dot(kva, wkn_ref[:, cols])
            v_all = _dot(kva, wv_ref[:, cols])
        qn, qr, kn = qn_all[:, sl], qr_all[:, sl], kn_all[:, sl]
        q_inv = lax.rsqrt(_dot((qn * qn + qr * qr).astype(BF16), ones) * (1.0 / width) + EPS)
        k_inv = lax.rsqrt(_dot((kn * kn + kr_sq).astype(BF16), ones) * (1.0 / width) + EPS)
        q_ref[0, h, :LANES, :] = (qn * q_inv * gqn_ref[...]).T.astype(BF16)
        q_ref[0, h, LANES:, :] = (rope(qr * q_inv * gqr_ref[...]) + qb_ref[...]).T.astype(BF16)
        k_ref[0, h, :, :LANES] = (kn * k_inv * gkn_ref[...]).astype(BF16)
        k_ref[0, h, :, LANES:] = (kr_rot * k_inv + kb_ref[...]).astype(BF16)
        v_ref[0, h] = v_all[:, sl].T.astype(BF16)


def _mla_prep(z, gqa, gkva, wqn, wqr, wkn, wv, l, gqn, gqr, gkn, gkr, qb, kb, tabs, B, S, tm):
    T = z.shape[0]
    tpb = S // tm
    H = MLA_HEADS
    full = lambda a: pl.BlockSpec(a.shape, lambda i: (0,) * a.ndim)
    layer = lambda a: pl.BlockSpec((None,) + a.shape[1:], lambda i: (l, 0, 0))
    row = pl.BlockSpec((tm, LANES), lambda i: (i, 0))
    return pl.pallas_call(
        _mla_prep_kernel,
        grid=(T // tm,),
        in_specs=[pl.BlockSpec((tm, MLA_Q_RANK), lambda i: (i, C_QA // MLA_Q_RANK)),
                  pl.BlockSpec((tm, MLA_KV_RANK), lambda i: (i, C_KVA // MLA_KV_RANK)),
                  pl.BlockSpec((tm, LANES), lambda i: (i, C_KR // LANES)),
                  full(gqa), full(gkva), layer(wqn), layer(wqr), layer(wkn), layer(wv),
                  full(gqn), full(gqr), full(gkn), full(gkr), full(qb), full(kb), row, row, row],
        out_specs=[pl.BlockSpec((1, H, QK_PAD, tm), lambda i: (i // tpb, 0, 0, i % tpb)),
                   pl.BlockSpec((1, H, tm, QK_PAD), lambda i: (i // tpb, 0, i % tpb, 0)),
                   pl.BlockSpec((1, H, MLA_V, tm), lambda i: (i // tpb, 0, 0, i % tpb))],
        out_shape=[jax.ShapeDtypeStruct((B, H, QK_PAD, S), BF16),
                   jax.ShapeDtypeStruct((B, H, S, QK_PAD), BF16),
                   jax.ShapeDtypeStruct((B, H, MLA_V, S), BF16)],
        compiler_params=_cp("parallel"),
        name="mla_prep",
    )(z, z, z, gqa, gkva, wqn, wqr, wkn, wv, gqn, gqr, gkn, gkr, qb, kb, *tabs)


def _gelu_tanh(x):
    return 0.5 * x * (1.0 + jnp.tanh(0.7978845608028654 * (x + 0.044715 * (x * x * x))))


def _sg_kernel(zu_ref, zv_ref, g_ref, w_ref, b_ref, o_ref):
    tm = zu_ref.shape[0]
    r = lax.broadcasted_iota(jnp.int32, (SG_LEN, SG_LEN), 0)
    c = lax.broadcasted_iota(jnp.int32, (SG_LEN, SG_LEN), 1)
    causal = c <= r
    for g in range(SG_GROUPS):
        sl = slice(g * SG_CH, (g + 1) * SG_CH)
        wt = jnp.where(causal, w_ref[g], 0.0).astype(BF16)
        v = _gelu_tanh(zv_ref[:, sl].astype(F32))
        v = (v * _rms_scale(v, SG_CH) * g_ref[:, sl]).astype(BF16)
        u = _gelu_tanh(zu_ref[:, sl].astype(F32))
        for n in range(tm // SG_LEN):
            rows = slice(n * SG_LEN, (n + 1) * SG_LEN)
            s = _dot(wt, v[rows, :]) + b_ref[:, sl]
            o_ref[rows, sl] = (u[rows, :] * s).astype(BF16)


def _sg(z, gain, w, bias, tm):
    T = z.shape[0]
    return pl.pallas_call(
        _sg_kernel,
        grid=(T // tm,),
        in_specs=[pl.BlockSpec((tm, SG_WIDTH), lambda i: (i, C_SGU // SG_WIDTH)),
                  pl.BlockSpec((tm, SG_WIDTH), lambda i: (i, C_SGV // SG_WIDTH)),
                  pl.BlockSpec((1, SG_WIDTH), lambda i: (0, 0)),
                  pl.BlockSpec((SG_GROUPS, SG_LEN, SG_LEN), lambda i: (0, 0, 0)),
                  pl.BlockSpec((SG_LEN, SG_WIDTH), lambda i: (0, 0))],
        out_specs=pl.BlockSpec((tm, SG_WIDTH), lambda i: (i, 0)),
        out_shape=jax.ShapeDtypeStruct((T, SG_WIDTH), BF16),
        compiler_params=_cp("parallel"),
        name="spatial_gating",
    )(z, z, gain, w, bias)


def _flash_t(qt, k_ref, vt_ref, n_off, diag_fixes, online):
    n = qt.shape[1]

    def step(j, state, fix=None):
        m, l, acc = state
        start = pl.multiple_of(j * KEY_TILE, KEY_TILE)
        s = _dot(k_ref[0, 0, pl.ds(start, KEY_TILE), :], qt)
        if fix is not None:
            s = fix(s)
        if online:
            m_new = jnp.maximum(m, jnp.max(s, axis=0, keepdims=True))
            alpha = jnp.exp2(m - m_new)
            p = jnp.exp2(s - m_new)
            l = alpha * l
            acc = alpha * acc
        else:
            m_new = m
            p = jnp.exp2(s)
        l = l + jnp.sum(p, axis=0, keepdims=True)
        acc = acc + _dot(vt_ref[0, 0, :, pl.ds(start, KEY_TILE)], p.astype(BF16))
        return m_new, l, acc

    def run(j0, count, state):
        for u in range(count):
            state = step(j0 + u, state)
        return state

    state = (jnp.full((1, n), NEG_BIG, F32), jnp.zeros((1, n), F32), jnp.zeros((DA_V, n), F32))
    n_grp = n_off // KV_UNROLL
    state = lax.fori_loop(0, n_grp, lambda g, st: run(g * KV_UNROLL, KV_UNROLL, st), state)
    state = lax.fori_loop(n_grp * KV_UNROLL, n_off, lambda j, st: run(j, 1, st), state)
    for t, fix in enumerate(diag_fixes):
        state = step(n_off + t, state, fix)
    _, l, acc = state
    return acc / l


def _chunk_visible(n, tq, key_off):
    r = lax.broadcasted_iota(jnp.int32, (KEY_TILE, n), 0) + key_off
    c = lax.broadcasted_iota(jnp.int32, (KEY_TILE, n), 1)
    if n > tq:
        c = jnp.where(c >= tq, c - tq, c)
    shift = CHUNK.bit_length() - 1
    return lax.shift_right_logical(r, shift) <= lax.shift_right_logical(c, shift)


def _da_attn_kernel(q_ref, k_ref, vt_ref, dk_ref, dq_ref, sl2_ref, lam_ref, hg_ref, o_ref, *,
                    tq, lam_init, online):
    qi = pl.program_id(2)
    qf = q_ref[0, 0]
    row = lax.broadcasted_iota(jnp.int32, qf.shape, 0)
    zero = jnp.zeros_like(qf)
    qt = jnp.concatenate([jnp.where((row >= DA_QK) & (row < 2 * DA_QK), zero, qf),
                          jnp.where(row < DA_QK, zero, qf)], axis=1)

    def fix(s):
        dq = dq_ref[0]
        dq2 = jnp.concatenate([dq, dq], axis=1)
        corr = sl2_ref[0, :, :1] * jnp.maximum(dk_ref[...] - dq2, 0.0)
        return jnp.where(_chunk_visible(2 * tq, tq, 0), s - corr, NEG_BIG)

    ot = _flash_t(qt, k_ref, vt_ref, qi, [fix], online)
    lv = lam_ref[...]
    lam = (jnp.exp(jnp.sum(lv[0:1] * lv[1:2], axis=-1, keepdims=True))
           - jnp.exp(jnp.sum(lv[2:3] * lv[3:4], axis=-1, keepdims=True)) + lam_init)
    od = ot[:, :tq] - lam * ot[:, tq:]
    inv = lax.rsqrt(jnp.sum(od * od, axis=0, keepdims=True) * (1.0 / DA_V) + EPS)
    od = od * inv * (1.0 - lam_init)
    o_ref[...] = (od.T * hg_ref[0]).astype(o_ref.dtype)


def _da_attention(q, k, vt, d_col, d_row, sl2, lam_vecs, hg, B, S, tq, lam_init, online):
    H = DA_HEADS
    nq = S // tq
    assert tq == KEY_TILE
    return pl.pallas_call(
        functools.partial(_da_attn_kernel, tq=tq, lam_init=lam_init, online=online),
        grid=(B, H, nq),
        in_specs=[pl.BlockSpec((1, 1, QK_PAD, tq), lambda b, h, i: (b, h, 0, i)),
                  pl.BlockSpec((1, 1, S, QK_PAD), lambda b, h, i: (b, h, 0, 0)),
                  pl.BlockSpec((1, 1, DA_V, S), lambda b, h, i: (b, h, 0, 0)),
                  pl.BlockSpec((tq, 1), lambda b, h, i: (b * nq + i, 0)),
                  pl.BlockSpec((1, 1, tq), lambda b, h, i: (b, 0, i)),
                  pl.BlockSpec((1, 1, LANES), lambda b, h, i: (h, 0, 0)),
                  pl.BlockSpec((4, DA_QK), lambda b, h, i: (0, 0)),
                  pl.BlockSpec((1, 1, DA_V), lambda b, h, i: (h, 0, 0))],
        out_specs=pl.BlockSpec((tq, DA_V), lambda b, h, i: (b * nq + i, h)),
        out_shape=jax.ShapeDtypeStruct((B * S, DA_WIDTH), BF16),
        compiler_params=_cp("parallel", "parallel", "parallel"),
        name="da_attention",
    )(q, k, vt, d_col, d_row, sl2, lam_vecs, hg)


def _mla_attn_kernel(q_ref, k_ref, vt_ref, o_ref, *, tq, online):
    qi = pl.program_id(2)
    tiles = tq // KEY_TILE
    fixes = [lambda s, t=t: jnp.where(_chunk_visible(tq, tq, t * KEY_TILE), s, NEG_BIG) for t in range(tiles)]
    ot = _flash_t(q_ref[0, 0], k_ref, vt_ref, qi * tiles, fixes, online)
    o_ref[...] = ot.T.astype(o_ref.dtype)


def _mla_attention(q, k, vt, B, S, tq, online):
    H = MLA_HEADS
    nq = S // tq
    return pl.pallas_call(
        functools.partial(_mla_attn_kernel, tq=tq, online=online),
        grid=(B, H, nq),
        in_specs=[pl.BlockSpec((1, 1, QK_PAD, tq), lambda b, h, i: (b, h, 0, i)),
                  pl.BlockSpec((1, 1, S, QK_PAD), lambda b, h, i: (b, h, 0, 0)),
                  pl.BlockSpec((1, 1, MLA_V, S), lambda b, h, i: (b, h, 0, 0))],
        out_specs=pl.BlockSpec((tq, MLA_V), lambda b, h, i: (b * nq + i, h)),
        out_shape=jax.ShapeDtypeStruct((B * S, MLA_WIDTH), BF16),
        compiler_params=_cp("parallel", "parallel", "parallel"),
        name="mla_attention",
    )(q, k, vt)


def _out_proj_kernel(a_ref, b_ref, c_ref, w_ref, x_ref, g_ref, o_ref):
    ka, kb = a_ref.shape[1], a_ref.shape[1] + b_ref.shape[1]
    mix = (_dot(a_ref[...], w_ref[:ka, :].astype(BF16)) + _dot(b_ref[...], w_ref[ka:kb, :].astype(BF16))
           + _dot(c_ref[...], w_ref[kb:, :].astype(BF16)))
    o_ref[...] = x_ref[...] + g_ref[0] * mix


def _out_proj(a, b, c, w, l, x2, mod_l, S, tm, tn):
    T, D = x2.shape
    tpb = S // tm
    nb = D // tn
    w_mode = dict(pipeline_mode=pl.Buffered(1)) if nb == 1 else {}
    return pl.pallas_call(
        _out_proj_kernel,
        grid=(T // tm, nb),
        in_specs=[pl.BlockSpec((tm, DA_WIDTH), lambda i, j: (i, 0)),
                  pl.BlockSpec((tm, MLA_WIDTH), lambda i, j: (i, 0)),
                  pl.BlockSpec((tm, SG_WIDTH), lambda i, j: (i, 0)),
                  pl.BlockSpec((None, w.shape[1], tn), lambda i, j: (l, 0, j), **w_mode),
                  pl.BlockSpec((tm, tn), lambda i, j: (i, j)),
                  pl.BlockSpec((1, 1, tn), lambda i, j: (i // tpb, 0, 2 * nb + j))],
        out_specs=pl.BlockSpec((tm, tn), lambda i, j: (i, j)),
        out_shape=jax.ShapeDtypeStruct((T, D), F32),
        compiler_params=_cp("parallel", "parallel"),
        name="out_proj",
    )(a, b, c, w, x2, mod_l)


def _ffn_up_kernel(x_ref, xh_ref, sh_ref, sc_ref, wg_ref, wv_ref, cwg_ref, cwv_ref, cbg_ref, cbv_ref,
                   a_ref, h_scr, *, tpb):
    i = pl.program_id(0)
    j = pl.program_id(1)

    @pl.when(j == 0)
    def _():
        def modnorm(x):
            return x * _rms_scale(x, x.shape[-1]) * (1.0 + sc_ref[0]) + sh_ref[0]

        for r in range(0, x_ref.shape[0], NORM_ROWS):
            h_scr[HALO + r:HALO + r + NORM_ROWS, :] = modnorm(x_ref[r:r + NORM_ROWS, :]).astype(BF16)
        keep = jnp.where(i % tpb == 0, 0.0, 1.0)
        h_scr[:HALO, :] = (modnorm(xh_ref[...]) * keep).astype(BF16)

    h = h_scr[...]

    def conv(a, cw_ref, cb_ref):
        y = (cb_ref[...] + a * cw_ref[2:3, :] + pltpu.roll(a, 1, 0) * cw_ref[1:2, :]
             + pltpu.roll(a, 2, 0) * cw_ref[0:1, :])
        return y[HALO:, :]

    yg = conv(_dot(h, wg_ref[...]), cwg_ref, cbg_ref)
    yv = conv(_dot(h, wv_ref[...]), cwv_ref, cbv_ref)
    a_ref[...] = (yg / (1.0 + jnp.exp(-yg)) * yv).astype(BF16)


def _ffn_down_kernel(a_ref, w_ref, x_ref, g_ref, o_ref, wb_scr):
    @pl.when(pl.program_id(1) == 0)
    def _():
        for r in range(0, w_ref.shape[0], DOWN_CAST_ROWS):
            wb_scr[r:r + DOWN_CAST_ROWS, :] = w_ref[r:r + DOWN_CAST_ROWS, :].astype(BF16)

    o_ref[...] = x_ref[...] + g_ref[0] * _dot(a_ref[...], wb_scr[...])


def _ffn(x2, mod_l, w_up, conv_w, conv_b, w_down, l, S, tm, tf):
    T, D = x2.shape
    F = w_down.shape[1]
    nf = F // tf
    tpb = S // tm
    hb = tm // HALO
    act = pl.pallas_call(
        functools.partial(_ffn_up_kernel, tpb=tpb),
        grid=(T // tm, nf),
        in_specs=[pl.BlockSpec((tm, D), lambda i, j: (i, 0)),
                  pl.BlockSpec((HALO, D), lambda i, j: (jnp.maximum(i * hb - 1, 0), 0)),
                  pl.BlockSpec((1, 1, D), lambda i, j: (i // tpb, 0, 3)),
                  pl.BlockSpec((1, 1, D), lambda i, j: (i // tpb, 0, 4)),
                  pl.BlockSpec((None, D, tf), lambda i, j: (l, 0, j)),
                  pl.BlockSpec((None, D, tf), lambda i, j: (l, 0, nf + j)),
                  pl.BlockSpec((None, CONV_W, tf), lambda i, j: (l, 0, j)),
                  pl.BlockSpec((None, CONV_W, tf), lambda i, j: (l, 0, nf + j)),
                  pl.BlockSpec((None, 1, tf), lambda i, j: (l, 0, j)),
                  pl.BlockSpec((None, 1, tf), lambda i, j: (l, 0, nf + j))],
        out_specs=pl.BlockSpec((tm, tf), lambda i, j: (i, j)),
        out_shape=jax.ShapeDtypeStruct((T, F), BF16),
        scratch_shapes=[pltpu.VMEM((HALO + tm, D), BF16)],
        compiler_params=_cp("parallel", "arbitrary"),
        name="conv_ffn_up",
    )(x2, x2, mod_l, mod_l, w_up, w_up, conv_w, conv_w, conv_b, conv_b)

    td, tn = ROW_TILE, DOWN_COLS
    nb = D // tn
    tpd = S // td
    return pl.pallas_call(
        _ffn_down_kernel,
        grid=(nb, T // td),
        in_specs=[pl.BlockSpec((td, F), lambda j, i: (i, 0)),
                  pl.BlockSpec((None, F, tn), lambda j, i: (l, 0, j)),
                  pl.BlockSpec((td, tn), lambda j, i: (i, j)),
                  pl.BlockSpec((1, 1, tn), lambda j, i: (i // tpd, 0, 5 * nb + j))],
        out_specs=pl.BlockSpec((td, tn), lambda j, i: (i, j)),
        out_shape=jax.ShapeDtypeStruct((T, D), F32),
        scratch_shapes=[pltpu.VMEM((F, tn), BF16)],
        compiler_params=_cp("parallel", "arbitrary"),
        name="ffn_down",
    )(act, w_down, x2, mod_l)


def _split3(v):
    a = v.astype(BF16).astype(F32)
    b = (v - a).astype(BF16).astype(F32)
    c = (v - a - b).astype(BF16).astype(F32)
    return a, b, c


def kernel(x, c, positions, w_ada, b_ada, w_in, da_q_gain, da_k_gain, da_lq1, da_lk1, da_lq2, da_lk2, da_head_gain, mla_q_a_gain, mla_w_uq, mla_kv_a_gain, mla_w_ukv, mla_q_gain, mla_k_gain, sg_v_gain, sg_w, sg_b, w_out, ffn_w_up, ffn_conv_w, ffn_conv_b, ffn_w_down):
    B, S, D = x.shape
    L = w_ada.shape[0]
    T = B * S
    assert S % ROW_TILE == 0 and S % KEY_TILE == 0
    tm_big = min(S, FFN_ROWS)
    tm = ROW_TILE
    tq_mla = min(S, MLA_QUERIES)

    mod = _ada_mod(c, w_ada, b_ada)

    d = (positions - positions[:, :1]).astype(F32)
    d_col = d.reshape(T, 1)
    d_row = d.reshape(B, 1, S)
    tabs = _rope_tables(positions.astype(F32).reshape(T, 1))

    H = DA_HEADS
    sl = jnp.asarray(ALIBI_SLOPES, F32) * LOG2E
    slope6 = jnp.tile(jnp.stack(_split3(sl), axis=-1), (1, 2))
    zeros = lambda n: jnp.zeros((H, n), F32)
    kc = jnp.concatenate([zeros(6), slope6, jnp.ones((H, 3), F32), zeros(LANES - 15)], axis=-1)
    kc = kc.reshape(H, 1, LANES)
    sl2 = jnp.broadcast_to((2.0 * sl)[:, None, None], (H, 1, LANES))

    da_scale = DA_QK ** -0.5 * LOG2E
    mla_width = MLA_NOPE + MLA_ROPE
    mla_scale = mla_width ** -0.5 * LOG2E
    pad_r = lambda g: jnp.pad(g, (0, LANES - MLA_ROPE)).reshape(1, LANES)
    shift_lanes = lambda v: jnp.pad(v, (MLA_ROPE, LANES - MLA_ROPE - 3)).reshape(1, LANES)
    kb_mla = shift_lanes(jnp.ones((3,), F32))
    gmax = lambda g: jnp.max(jnp.abs(g))

    in_moves = [(C_QA, 2304, 512), (C_SGU, 3136, 512), (C_SGV, 3648, 512), (C_DAQ, 0, 768),
                (C_DAK, 768, 768), (C_DAV, 1536, 768), (C_KVA, 2816, 256), (C_KR, 3072, MLA_ROPE)]
    w_in_p = _row_relayout(jnp.swapaxes(w_in, 1, 2), in_moves, IN_PAD, 512)
    heads = range(MLA_HEADS)
    wqn_all, wqr_all = _relayout(
        mla_w_uq,
        [[(h * LANES, h * mla_width, MLA_NOPE) for h in heads],
         [(h * LANES, h * mla_width + MLA_NOPE, MLA_ROPE) for h in heads]],
        [MLA_WIDTH, MLA_HEADS * LANES], 256)
    kv_width = MLA_NOPE + MLA_V
    wkn_all, wv_all = _relayout(
        mla_w_ukv,
        [[(h * LANES, h * kv_width, MLA_NOPE) for h in heads],
         [(h * LANES, h * kv_width + MLA_NOPE, MLA_V) for h in heads]],
        [MLA_WIDTH, MLA_WIDTH], 256)
    w_up_b = ffn_w_up.astype(BF16)
    conv_b3 = ffn_conv_b.reshape(L, 1, -1)

    x2 = x.reshape(T, D)
    for l in range(L):
        mod_l = mod[l].reshape(B, 1, 6 * D)
        z = _norm_matmul(x2, mod_l, 0, w_in_p, l, S, tm, IN_PAD)

        lam_init = 0.8 - 0.6 * math.exp(-0.3 * l)
        gq = (jnp.tile(da_q_gain[l], 2) * da_scale).reshape(1, LANES)
        gk = jnp.tile(da_k_gain[l], 2).reshape(1, LANES)
        bound = BOUND_SLACK * DA_QK * da_scale * gmax(da_q_gain[l]) * gmax(da_k_gain[l])
        nb = jnp.broadcast_to(-jnp.stack(_split3(bound)), (H, 3))
        qc = jnp.concatenate([slope6, zeros(6), nb, zeros(LANES - 15)], axis=-1).reshape(H, 1, LANES)
        q_da, k_da, vt_da = _da_prep(z, d_col, gq, gk, qc, kc, B, S, tm)
        lam_vecs = jnp.stack([da_lq1[l], da_lk1[l], da_lq2[l], da_lk2[l]])
        da_args = (q_da, k_da, vt_da, d_col, d_row, sl2, lam_vecs, da_head_gain[l].reshape(H, 1, DA_V))
        out_a = lax.cond(
            2.0 * bound <= MAX_SHIFT,
            lambda a: _da_attention(*a, B, S, KEY_TILE, lam_init, online=False),
            lambda a: _da_attention(*a, B, S, KEY_TILE, lam_init, online=True), da_args)

        qg, kg = mla_q_gain[l], mla_k_gain[l]
        bound_m = BOUND_SLACK * mla_width * mla_scale * gmax(qg) * gmax(kg)
        q_m, k_m, v_m = _mla_prep(
            z, mla_q_a_gain[l].reshape(1, -1), mla_kv_a_gain[l].reshape(1, -1),
            wqn_all, wqr_all, wkn_all, wv_all, l,
            (qg[:MLA_NOPE] * mla_scale).reshape(1, LANES), pad_r(qg[MLA_NOPE:] * mla_scale),
            kg[:MLA_NOPE].reshape(1, LANES), pad_r(kg[MLA_NOPE:]),
            shift_lanes(-jnp.stack(_split3(bound_m))), kb_mla, tabs, B, S, tm)
        out_b = lax.cond(
            2.0 * bound_m <= MAX_SHIFT,
            lambda a: _mla_attention(*a, B, S, tq_mla, online=False),
            lambda a: _mla_attention(*a, B, S, tq_mla, online=True), (q_m, k_m, v_m))

        sg_bias = jnp.repeat(sg_b[l].T, SG_CH, axis=1)
        out_c = _sg(z, sg_v_gain[l].reshape(1, SG_WIDTH), sg_w[l], sg_bias, tm)

        x2 = _out_proj(out_a, out_b, out_c, w_out, l, x2, mod_l, S, tm, D)
        x2 = _ffn(x2, mod_l, w_up_b, ffn_conv_w, conv_b3, ffn_w_down, l, S, tm_big, FFN_TILE)
    return x2.reshape(B, S, D)
```
